```python
import functools
import jax
import jax.numpy as jnp
from jax import lax
import numpy as np

D_MODEL = 1024
BATCH = 8
SEQ = 8192
DEPTH = 1
DEC_BATCH = 128
DEC_SEQ = 8
PAST_LEN = 8192
PAGE_SIZE = 128

MIX_GLA = D_MODEL // 2
MIX_NSA = D_MODEL - MIX_GLA
GLA_HEADS = 4
GLA_DV = MIX_GLA // GLA_HEADS
GLA_DK = GLA_DV // 2
GLA_RANK = 16
GLA_TAU = 16.0
GLA_CHUNK = 64
NSA_HEADS = 8
NSA_GROUPS = 2
NSA_HPG = NSA_HEADS // NSA_GROUPS
NSA_HD = MIX_NSA // NSA_HEADS
CMP_LEN = 32
CMP_STRIDE = 16
CMP_HIDDEN = 128
SLC_LEN = 64
SLC_TOP = 16
WINDOW = 512
Q_BLK = 128
FORCE_SCORE = 1e9
NEG_BIG = -1e30
N_EXPERT_GROUPS = 4
EXPERTS_PER_GROUP = 8
N_EXPERTS = N_EXPERT_GROUPS * EXPERTS_PER_GROUP
EXPERT_TOPK = 2
D_EXPERT = 512
DEEPNORM_ALPHA = (2.0 * DEPTH) ** 0.25
DEEPNORM_BETA = (8.0 * DEPTH) ** -0.25
LN_EPS = 1e-5
RMS_EPS = 1e-6
IN_SIZES = (GLA_HEADS * GLA_DK, GLA_HEADS * GLA_DK, GLA_HEADS * GLA_DV, GLA_RANK, GLA_HEADS * GLA_DV,
            NSA_HEADS * NSA_HD, 6 * NSA_GROUPS * NSA_HD, 3 * NSA_HEADS)
D_IN = sum(IN_SIZES)

kernel_name = 'hybrid_gla_nsa_hmoe_step'


def layer_norm(x, g, b):
    xf = x.astype(jnp.float32)
    mu = xf.mean(-1, keepdims=True)
    var = jnp.mean(jnp.square(xf - mu), -1, keepdims=True)
    return ((xf - mu) * lax.rsqrt(var + LN_EPS) * g + b).astype(x.dtype)


def rms_norm(x, g):
    xf = x.astype(jnp.float32)
    return (xf * lax.rsqrt(jnp.mean(xf * xf, -1, keepdims=True) + RMS_EPS) * g).astype(x.dtype)


def masked_softmax(s, mask):
    s = jnp.where(mask, s.astype(jnp.float32), NEG_BIG)
    e = jnp.exp(s - s.max(-1, keepdims=True)) * mask
    return e / jnp.maximum(e.sum(-1, keepdims=True), 1e-30)


def gla_recurrent(q, k, v, log_a, s0):
    b, t, h, _ = q.shape
    dv = v.shape[-1]
    f32 = jnp.float32
    n_chunks = -(-t // GLA_CHUNK)
    pad = n_chunks * GLA_CHUNK - t

    def to_chunks(z):
        z = jnp.pad(z.astype(f32), ((0, 0), (0, pad), (0, 0), (0, 0)))
        return z.reshape(b, n_chunks, GLA_CHUNK, h, z.shape[-1]).transpose(1, 0, 3, 2, 4)

    causal = jnp.tril(jnp.ones((GLA_CHUNK, GLA_CHUNK), bool))

    def step(s, inp):
        qc, kc, vc, ac = inp
        cum = jnp.cumsum(ac, axis=2)
        last = cum[:, :, -1:]
        q_dec = qc * jnp.exp(cum)
        k_inv = kc * jnp.exp(-cum)
        att = jnp.where(causal, jnp.einsum('bhcd,bhsd->bhcs', q_dec, k_inv), 0.0)
        o = jnp.einsum('bhcs,bhsv->bhcv', att, vc) + jnp.einsum('bhcd,bhdv->bhcv', q_dec, s)
        s = jnp.exp(last[:, :, 0, :, None]) * s + jnp.einsum('bhcd,bhcv->bhdv', kc * jnp.exp(last - cum), vc)
        return s, o

    s_fin, o = lax.scan(step, s0.astype(f32), tuple(map(to_chunks, (q, k, v, log_a))))
    o = o.transpose(1, 0, 3, 2, 4).reshape(b, n_chunks * GLA_CHUNK, h, dv)[:, :t]
    return o.astype(v.dtype), s_fin.astype(s0.dtype)


def compress(kv, pe, w1, w2):
    nc = (kv.shape[0] - CMP_LEN) // CMP_STRIDE + 1
    idx = np.arange(nc)[:, None] * CMP_STRIDE + np.arange(CMP_LEN)[None, :]
    blk = kv[idx] + pe[None, :, None, :]
    blk = blk.transpose(0, 2, 1, 3).reshape(nc, NSA_GROUPS, CMP_LEN * NSA_HD)
    return jax.nn.gelu(blk @ w1) @ w2


def prep_sparse(kv4, cmp_pe, cmp_w1, cmp_w2):
    t = kv4.shape[0]
    kc = compress(kv4[:, 0], cmp_pe[0], cmp_w1[0], cmp_w2[0])
    vc = compress(kv4[:, 1], cmp_pe[1], cmp_w1[1], cmp_w2[1])
    nc = kc.shape[0]
    starts = np.arange(nc) * CMP_STRIDE
    ns = -(-t // SLC_LEN)
    sel_starts = np.arange(ns) * SLC_LEN
    agg = jnp.asarray((starts[:, None] < sel_starts[None, :] + SLC_LEN)
                      & (starts[:, None] + CMP_LEN > sel_starts[None, :]), jnp.float32)
    cmp_end = jnp.asarray(starts + CMP_LEN - 1, jnp.int32)
    slc = jnp.pad(kv4[:, 2:], ((0, ns * SLC_LEN - t), (0, 0), (0, 0), (0, 0)))
    slc = slc.reshape(ns, SLC_LEN, 2, NSA_GROUPS, NSA_HD)
    return kc, vc, cmp_end, slc[:, :, 0], slc[:, :, 1], agg


def nsa_block(q, pos, gate, sparse, k_win, v_win, pos_win):
    kc, vc, cmp_end, k_slc, v_slc, agg = sparse
    f32 = jnp.float32
    nq = q.shape[0]
    qg = q.reshape(nq, NSA_GROUPS, NSA_HPG, NSA_HD).astype(f32) * (NSA_HD ** -0.5)
    valid_c = (cmp_end[None, :] <= pos[:, None])[:, None, None, :]
    p_c = masked_softmax(jnp.einsum('qghd,cgd->qghc', qg, kc.astype(f32)), valid_c)
    o_c = jnp.einsum('qghc,cgd->qghd', p_c, vc.astype(f32))
    ns = k_slc.shape[0]
    imp = jnp.einsum('qghc,cs->qgs', p_c, agg)
    cur = pos // SLC_LEN
    blk = jnp.arange(ns)
    forced = (blk[None, :] == 0) | (blk[None, :] == cur[:, None]) | (blk[None, :] == cur[:, None] - 1)
    imp = jnp.where(forced[:, None, :], FORCE_SCORE, imp)
    imp = jnp.where((blk[None, :] <= cur[:, None])[:, None, :], imp, -jnp.inf)
    _, sel = lax.top_k(imp, min(SLC_TOP, ns))
    g_idx = jnp.arange(NSA_GROUPS)[None, :, None]
    kb = k_slc.transpose(2, 0, 1, 3)[g_idx, sel].astype(f32)
    vb = v_slc.transpose(2, 0, 1, 3)[g_idx, sel].astype(f32)
    key_pos = sel[..., None] * SLC_LEN + jnp.arange(SLC_LEN)
    valid_s = (key_pos <= pos[:, None, None, None]).reshape(nq, NSA_GROUPS, 1, -1)
    s_s = jnp.einsum('qghd,qgnld->qghnl', qg, kb).reshape(nq, NSA_GROUPS, NSA_HPG, -1)
    p_s = masked_softmax(s_s, valid_s)
    o_s = jnp.einsum('qghm,qgmd->qghd', p_s, vb.reshape(nq, NSA_GROUPS, -1, NSA_HD))
    valid_w = ((pos_win[None, :] <= pos[:, None]) & (pos[:, None] - pos_win[None, :] < WINDOW)
               & (pos_win[None, :] >= 0))[:, None, None, :]
    p_w = masked_softmax(jnp.einsum('qghd,wgd->qghw', qg, k_win.astype(f32)), valid_w)
    o_w = jnp.einsum('qghw,wgd->qghd', p_w, v_win.astype(f32))
    g = gate.reshape(nq, NSA_GROUPS, NSA_HPG, 3).astype(f32)
    o = g[..., 0:1] * o_c + g[..., 1:2] * o_s + g[..., 2:3] * o_w
    return o.reshape(nq, NSA_HEADS, NSA_HD).astype(q.dtype)


def nsa_prompt(q, gate, kv, cmp_pe, cmp_w1, cmp_w2):
    t = q.shape[1]
    w_rows = min(WINDOW, t)

    def one_seq(args):
        qs, gs, kvs = args
        sparse = prep_sparse(kvs[:, :4], cmp_pe, cmp_w1, cmp_w2)
        win = jnp.pad(kvs[:, 4:], ((WINDOW, 0), (0, 0), (0, 0), (0, 0)))

        def one_block(i):
            s0 = i * Q_BLK
            qb = lax.dynamic_slice_in_dim(qs, s0, Q_BLK)
            gb = lax.dynamic_slice_in_dim(gs, s0, Q_BLK)
            slab = lax.dynamic_slice_in_dim(win, s0, WINDOW + Q_BLK)
            pos = s0 + jnp.arange(Q_BLK)
            pos_w = s0 - WINDOW + jnp.arange(WINDOW + Q_BLK)
            return nsa_block(qb, pos, gb, sparse, slab[:, 0], slab[:, 1], pos_w)

        o = lax.map(one_block, jnp.arange(t // Q_BLK))
        return o.reshape(t, NSA_HEADS, NSA_HD), kvs[t - w_rows:, 4:]

    return lax.map(one_seq, (q, gate, kv))


def nsa_sample(q, gate, kv, cache, page_table, win_buf, cmp_pe, cmp_w1, cmp_w2):
    n_new = q.shape[1]

    def one_seq(args):
        qs, gs, kvs, pt, buf = args
        past = cache[pt].reshape(-1, 4, NSA_GROUPS, NSA_HD)
        past_len = past.shape[0]
        sparse = prep_sparse(jnp.concatenate([past, kvs[:, :4]], 0), cmp_pe, cmp_w1, cmp_w2)
        w_buf = buf.shape[0]
        slab = jnp.concatenate([buf, kvs[:, 4:]], 0)
        pos = past_len + jnp.arange(n_new)
        pos_w = past_len - w_buf + jnp.arange(w_buf + n_new)
        o = nsa_block(qs, pos, gs, sparse, slab[:, 0], slab[:, 1], pos_w)
        return o, slab[n_new:]

    return lax.map(one_seq, (q, gate, kv, page_table, win_buf))


def hier_moe(x, router_g_w, router_g_b, router_e_w, router_e_b, w_gate, w_up, w_down):
    b, t, d = x.shape
    f32 = jnp.float32
    xf = x.reshape(b * t, d)
    g_logits = (xf @ router_g_w + router_g_b).astype(f32)
    g_prob = jax.nn.softmax(g_logits, -1)
    g_top = jnp.argmax(g_logits, -1)
    p_group = jnp.take_along_axis(g_prob, g_top[:, None], -1)
    e_logits = jnp.einsum('nd,gde->nge', xf, router_e_w) + router_e_b
    e_logits = jnp.take_along_axis(e_logits, g_top[:, None, None], 1)[:, 0].astype(f32)
    top_p, top_i = lax.top_k(jax.nn.softmax(e_logits, -1), EXPERT_TOPK)
    top_p = top_p / top_p.sum(-1, keepdims=True)
    expert_id = g_top[:, None] * EXPERTS_PER_GROUP + top_i
    combine = jnp.einsum('nk,nke->ne', p_group * top_p, jax.nn.one_hot(expert_id, N_EXPERTS, dtype=f32))
    y = jnp.zeros((b * t, d), f32)
    for e in range(N_EXPERTS):
        hdn = jax.nn.silu(xf @ w_gate[e]) * (xf @ w_up[e])
        y = y + combine[:, e:e + 1] * (hdn @ w_down[e])
    return y.astype(x.dtype).reshape(b, t, d)


def decoder_layer(x, gla_s0, nsa_fn, w_in, gla_w_a, gla_b_a, nsa_gate_b, gla_norm, nsa_norm, w_o,
                  ln1_g, ln1_b, router_g_w, router_g_b, router_e_w, router_e_b, w_gate, w_up, w_down,
                  ln2_g, ln2_b):
    b, t, _ = x.shape
    offs = np.cumsum(IN_SIZES)[:-1].tolist()
    gq, gk, gv, glr, gog, nq, nkv, ngt = jnp.split(x @ w_in, offs, axis=-1)
    q = gq.reshape(b, t, GLA_HEADS, GLA_DK) * (GLA_DK ** -0.5)
    k = gk.reshape(b, t, GLA_HEADS, GLA_DK)
    v = gv.reshape(b, t, GLA_HEADS, GLA_DV)
    log_a = jax.nn.log_sigmoid((glr @ gla_w_a + gla_b_a).astype(jnp.float32)) / GLA_TAU
    o_gla, gla_state = gla_recurrent(q, k, v, log_a.reshape(b, t, GLA_HEADS, GLA_DK), gla_s0)
    o_gla = rms_norm(o_gla, gla_norm) * jax.nn.silu(gog.reshape(b, t, GLA_HEADS, GLA_DV))
    nsa_kv = nkv.reshape(b, t, 6, NSA_GROUPS, NSA_HD)
    gate = jax.nn.sigmoid(ngt + nsa_gate_b).reshape(b, t, NSA_HEADS, 3)
    o_nsa, win_state = nsa_fn(nq.reshape(b, t, NSA_HEADS, NSA_HD), gate, nsa_kv)
    o_nsa = rms_norm(o_nsa, nsa_norm)
    mix = jnp.concatenate([o_gla.reshape(b, t, MIX_GLA), o_nsa.reshape(b, t, MIX_NSA)], -1) @ w_o
    x = layer_norm(DEEPNORM_ALPHA * x + mix, ln1_g, ln1_b)
    ffn = hier_moe(x, router_g_w, router_g_b, router_e_w, router_e_b, w_gate, w_up, w_down)
    x = layer_norm(DEEPNORM_ALPHA * x + ffn, ln2_g, ln2_b)
    return x, nsa_kv[:, :, :4], win_state, gla_state


def setup_inputs(seed: int = 0) -> dict:
    key = jax.random.key(seed)
    keys = list(jax.random.split(key, 40))

    def nrm(shape, scale):
        return jax.random.normal(keys.pop(), shape, jnp.float32) * scale

    n_pages = PAST_LEN // PAGE_SIZE
    n_pool = (DEC_BATCH * n_pages * 5) // 4
    w_buf = min(WINDOW, PAST_LEN)
    perm = jax.random.permutation(keys.pop(), n_pool)
    page_table = perm[: DEC_BATCH * n_pages].reshape(DEC_BATCH, n_pages).astype(jnp.int32)
    L = DEPTH
    return {
        'x_prompt': nrm((BATCH, SEQ, D_MODEL), 1.0),
        'x_sample': nrm((DEC_BATCH, DEC_SEQ, D_MODEL), 1.0),
        'cache_nsa': nrm((L, n_pool, PAGE_SIZE, 4, NSA_GROUPS, NSA_HD), 1.0),
        'state_win': nrm((L, DEC_BATCH, w_buf, 2, NSA_GROUPS, NSA_HD), 1.0),
        'state_gla': nrm((L, DEC_BATCH, GLA_HEADS, GLA_DK, GLA_DV), 0.5),
        'page_table': page_table,
        'w_in': nrm((L, D_MODEL, D_IN), D_MODEL ** -0.5),
        'gla_w_a': nrm((L, GLA_RANK, GLA_HEADS * GLA_DK), GLA_RANK ** -0.5),
        'gla_b_a': nrm((L, GLA_HEADS * GLA_DK), 0.1),
        'nsa_gate_b': nrm((L, 3 * NSA_HEADS), 0.1),
        'cmp_pe': nrm((L, 2, CMP_LEN, NSA_HD), 0.1),
        'cmp_w1': nrm((L, 2, CMP_LEN * NSA_HD, CMP_HIDDEN), (CMP_LEN * NSA_HD) ** -0.5),
        'cmp_w2': nrm((L, 2, CMP_HIDDEN, NSA_HD), CMP_HIDDEN ** -0.5),
        'gla_norm': 1.0 + nrm((L, GLA_DV), 0.01),
        'nsa_norm': 1.0 + nrm((L, NSA_HD), 0.01),
        'w_o': nrm((L, D_MODEL, D_MODEL), DEEPNORM_BETA * D_MODEL ** -0.5),
        'ln1_g': 1.0 + nrm((L, D_MODEL), 0.01),
        'ln1_b': nrm((L, D_MODEL), 0.01),
        'router_g_w': nrm((L, D_MODEL, N_EXPERT_GROUPS), D_MODEL ** -0.5),
        'router_g_b': nrm((L, N_EXPERT_GROUPS), 0.01),
        'router_e_w': nrm((L, N_EXPERT_GROUPS, D_MODEL, EXPERTS_PER_GROUP), D_MODEL ** -0.5),
        'router_e_b': nrm((L, N_EXPERT_GROUPS, EXPERTS_PER_GROUP), 0.01),
        'w_gate': nrm((L, N_EXPERTS, D_MODEL, D_EXPERT), D_MODEL ** -0.5),
        'w_up': nrm((L, N_EXPERTS, D_MODEL, D_EXPERT), D_MODEL ** -0.5),
        'w_down': nrm((L, N_EXPERTS, D_EXPERT, D_MODEL), DEEPNORM_BETA * D_EXPERT ** -0.5),
        'ln2_g': 1.0 + nrm((L, D_MODEL), 0.01),
        'ln2_b': nrm((L, D_MODEL), 0.01),
    }


def reference(x_prompt, x_sample, cache_nsa, state_win, state_gla, page_table,
              w_in, gla_w_a, gla_b_a, nsa_gate_b, cmp_pe, cmp_w1, cmp_w2, gla_norm, nsa_norm, w_o,
              ln1_g, ln1_b, router_g_w, router_g_b, router_e_w, router_e_b, w_gate, w_up, w_down,
              ln2_g, ln2_b):
    xp, xs = x_prompt, x_sample
    kv_p, kv_s, win_p, win_s, gla_p, gla_s = [], [], [], [], [], []
    for l in range(DEPTH):
        lp = (w_in[l], gla_w_a[l], gla_b_a[l], nsa_gate_b[l], gla_norm[l], nsa_norm[l], w_o[l],
              ln1_g[l], ln1_b[l], router_g_w[l], router_g_b[l], router_e_w[l], router_e_b[l],
              w_gate[l], w_up[l], w_down[l], ln2_g[l], ln2_b[l])
        s0 = jnp.zeros((xp.shape[0], GLA_HEADS, GLA_DK, GLA_DV), state_gla.dtype)
        fn_p = functools.partial(nsa_prompt, cmp_pe=cmp_pe[l], cmp_w1=cmp_w1[l], cmp_w2=cmp_w2[l])
        xp, r_p, w_p, g_p = decoder_layer(xp, s0, fn_p, *lp)
        fn_s = functools.partial(nsa_sample, cache=cache_nsa[l], page_table=page_table, win_buf=state_win[l],
                                 cmp_pe=cmp_pe[l], cmp_w1=cmp_w1[l], cmp_w2=cmp_w2[l])
        xs, r_s, w_s, g_s = decoder_layer(xs, state_gla[l], fn_s, *lp)
        kv_p.append(r_p); kv_s.append(r_s)
        win_p.append(w_p); win_s.append(w_s)
        gla_p.append(g_p); gla_s.append(g_s)
    return (xp, xs, jnp.stack(kv_p), jnp.stack(kv_s), jnp.stack(win_p), jnp.stack(win_s),
            jnp.stack(gla_p), jnp.stack(gla_s))
```

```python
import functools

import numpy as np
import jax
import jax.numpy as jnp
from jax import lax
from jax.experimental import pallas as pl
from jax.experimental.pallas import tpu as pltpu

F32 = jnp.float32
BF16 = jnp.bfloat16

D_MODEL = 1024
GLA_HEADS = 4
GLA_DK = 64
GLA_DV = 128
GLA_RANK = 16
GLA_TAU = 16.0
GLA_CHUNK = 64
NSA_HEADS = 8
NSA_GROUPS = 2
NSA_HPG = 4
NSA_HD = 64
CMP_LEN = 32
CMP_STRIDE = 16
CMP_HIDDEN = 128
SLC_LEN = 64
SLC_TOP = 16
WINDOW = 512
PAGE_SIZE = 128
N_EXPERT_GROUPS = 4
EXPERTS_PER_GROUP = 8
N_EXPERTS = 32
D_EXPERT = 512
DEPTH = 1
DEEPNORM_ALPHA = (2.0 * DEPTH) ** 0.25
LN_EPS = 1e-5
RMS_EPS = 1e-6
NEG_BIG = -1e30
FORCE_SCORE = 1e9

LANES = 128
KV_TILE = 512
BLK_PER_TILE = KV_TILE // SLC_LEN
N_SLC_PAD = 128
VMEM_LIMIT = 56 * 1024 * 1024

SEG_QK, SEG_V, SEG_OG, SEG_NQ, SEG_ROWS, SEG_WIN, SEG_SM = 0, 512, 1024, 1536, 2048, 2560, 2816
D_PROJ = 2944
SM_GATE_OFF = GLA_RANK


def _cparams(sem):
    return pltpu.CompilerParams(dimension_semantics=sem, vmem_limit_bytes=VMEM_LIMIT)


def _dot(a, b):
    return jnp.dot(a, b, preferred_element_type=F32)


def _dot_nt(a, b):
    return lax.dot_general(a, b, (((1,), (1,)), ((), ())), preferred_element_type=F32)


def _dot_tn(a, b):
    return lax.dot_general(a, b, (((0,), (0,)), ((), ())), preferred_element_type=F32)


def _split(x):
    hi = x.astype(BF16)
    lo = (x - hi.astype(F32)).astype(BF16)
    return hi, lo


def _dot3(a, b_hi, b_lo):
    a_hi, a_lo = _split(a)
    return _dot(a_hi, b_hi) + (_dot(a_hi, b_lo) + _dot(a_lo, b_hi))


def _sigmoid(x):
    return 1.0 / (1.0 + jnp.exp(-x))


def _silu(x):
    return x * _sigmoid(x)


def _layer_norm(y, g, b):
    mu = jnp.mean(y, axis=-1, keepdims=True)
    d = y - mu
    var = jnp.mean(d * d, axis=-1, keepdims=True)
    return d * lax.rsqrt(var + LN_EPS) * g + b


def _proj_body(x_ref, w_ref, qk_ref, v_ref, og_ref, nq_ref, rows_ref, win_ref, kvb_ref, sm_ref):
    x = x_ref[...].astype(BF16)

    def seg(off, width):
        return _dot(x, w_ref[:, off:off + width])

    qk_ref[...] = seg(SEG_QK, 512)
    v_ref[...] = seg(SEG_V, 512)
    og_ref[...] = seg(SEG_OG, 512)
    nq_ref[...] = seg(SEG_NQ, 512)
    rows = seg(SEG_ROWS, 512)
    rows_ref[...] = rows
    win = seg(SEG_WIN, 256)
    win_ref[...] = win
    kvb_ref[:, 0:256] = rows[:, 256:512].astype(BF16)
    kvb_ref[:, 256:512] = win.astype(BF16)
    sm_ref[...] = seg(SEG_SM, 128)


def _proj(x2d, w_proj):
    n = x2d.shape[0]
    tm = min(n, 512)
    widths = (512, 512, 512, 512, 512, 256, 512, 128)
    dtypes = (F32, F32, F32, F32, F32, F32, BF16, F32)
    return pl.pallas_call(
        _proj_body,
        grid=(n // tm,),
        in_specs=[pl.BlockSpec((tm, D_MODEL), lambda i: (i, 0)),
                  pl.BlockSpec((D_MODEL, D_PROJ), lambda i: (0, 0))],
        out_specs=[pl.BlockSpec((tm, w), lambda i: (i, 0)) for w in widths],
        out_shape=[jax.ShapeDtypeStruct((n, w), d) for w, d in zip(widths, dtypes)],
        compiler_params=_cparams(("parallel",)),
        name="proj",
    )(x2d, w_proj)


def _gla_body(*refs, t_real, n_chunks, has_s0):
    if has_s0:
        qk_ref, v_ref, og_ref, sm_ref, wah_ref, wal_ref, ba_ref, gn_ref, s0_ref, o_ref, sf_ref, st_ref = refs
    else:
        qk_ref, v_ref, og_ref, sm_ref, wah_ref, wal_ref, ba_ref, gn_ref, o_ref, sf_ref, st_ref = refs
        s0_ref = None
    c_len = GLA_CHUNK
    t = pl.program_id(1)

    @pl.when(t == 0)
    def _():
        for h in range(GLA_HEADS):
            if has_s0:
                st_ref[h] = s0_ref[0, h].T
            else:
                st_ref[h] = jnp.zeros((GLA_DV, GLA_DK), F32)

    r_io = lax.broadcasted_iota(jnp.int32, (c_len, c_len), 0)
    c_io = lax.broadcasted_iota(jnp.int32, (c_len, c_len), 1)
    causal = r_io >= c_io
    tri = jnp.where(causal, 1.0, 0.0).astype(BF16)
    padded = t_real < c_len

    def load(ref, rows):
        if padded:
            x = ref[...]
            return jnp.concatenate([x, jnp.zeros((c_len - t_real, x.shape[1]), x.dtype)], axis=0)
        return ref[rows, :]

    for c in range(n_chunks):
        rows = pl.ds(c * c_len, c_len)
        qk = load(qk_ref, rows)
        v = load(v_ref, rows)
        sm = load(sm_ref, rows)
        z = _dot3(sm, wah_ref[...], wal_ref[...]) + ba_ref[...]
        la = (jnp.minimum(z, 0.0) - jnp.log1p(jnp.exp(-jnp.abs(z)))) / GLA_TAU
        if padded:
            row_id = lax.broadcasted_iota(jnp.int32, la.shape, 0)
            la = jnp.where(row_id < t_real, la, 0.0)
        la_hi, la_lo = _split(la)
        cum = _dot(tri, la_hi) + _dot(tri, la_lo)
        last = cum[c_len - 1:c_len, :]
        e_q = jnp.exp(cum)
        e_k = jnp.exp(-cum)
        e_kd = jnp.exp(last - cum)
        e_l = jnp.exp(last)
        for h in range(GLA_HEADS):
            sl = slice(h * GLA_DK, (h + 1) * GLA_DK)
            qh = qk[:, sl] * (GLA_DK ** -0.5)
            kh = qk[:, GLA_HEADS * GLA_DK + h * GLA_DK:GLA_HEADS * GLA_DK + (h + 1) * GLA_DK]
            vh = v[:, h * GLA_DV:(h + 1) * GLA_DV].astype(BF16)
            q_dec = (qh * e_q[:, sl]).astype(BF16)
            k_inv = (kh * e_k[:, sl]).astype(BF16)
            k_dec = (kh * e_kd[:, sl]).astype(BF16)
            att = jnp.where(causal, _dot_nt(q_dec, k_inv), 0.0).astype(BF16)
            s_t = st_ref[h]
            o = _dot(att, vh) + _dot_nt(q_dec, s_t.astype(BF16))
            st_ref[h] = s_t * e_l[:, sl] + _dot_tn(vh, k_dec)
            gate = _silu(load(og_ref, rows)[:, h * GLA_DV:(h + 1) * GLA_DV])
            on = o * lax.rsqrt(jnp.mean(o * o, axis=-1, keepdims=True) + RMS_EPS) * gn_ref[...]
            res = on * gate
            if padded:
                o_ref[:, h * GLA_DV:(h + 1) * GLA_DV] = res[:t_real]
            else:
                o_ref[rows, h * GLA_DV:(h + 1) * GLA_DV] = res

    @pl.when(t == pl.num_programs(1) - 1)
    def _():
        for h in range(GLA_HEADS):
            sf_ref[0, h] = st_ref[h].T


def _gla(qk, v, og, sm, wa_hi, wa_lo, ba, gn, s0, nb, t_len):
    n = nb * t_len
    if t_len >= GLA_CHUNK:
        tt = min(t_len, 512)
        t_real = GLA_CHUNK
        n_chunks = tt // GLA_CHUNK
    else:
        tt = t_len
        t_real = t_len
        n_chunks = 1
    nt = t_len // tt
    row = lambda b, t: (b * nt + t, 0)
    const = lambda b, t: (0, 0)
    in_specs = [pl.BlockSpec((tt, 512), row), pl.BlockSpec((tt, 512), row), pl.BlockSpec((tt, 512), row),
                pl.BlockSpec((tt, LANES), row),
                pl.BlockSpec((LANES, 256), const), pl.BlockSpec((LANES, 256), const),
                pl.BlockSpec((1, 256), const), pl.BlockSpec((1, GLA_DV), const)]
    args = [qk, v, og, sm, wa_hi, wa_lo, ba, gn]
    if s0 is not None:
        in_specs.append(pl.BlockSpec((1, GLA_HEADS, GLA_DK, GLA_DV), lambda b, t: (b, 0, 0, 0)))
        args.append(s0)
    return pl.pallas_call(
        functools.partial(_gla_body, t_real=t_real, n_chunks=n_chunks, has_s0=s0 is not None),
        grid=(nb, nt),
        in_specs=in_specs,
        out_specs=[pl.BlockSpec((tt, 512), row),
                   pl.BlockSpec((1, GLA_HEADS, GLA_DK, GLA_DV), lambda b, t: (b, 0, 0, 0))],
        out_shape=[jax.ShapeDtypeStruct((n, 512), F32),
                   jax.ShapeDtypeStruct((nb, GLA_HEADS, GLA_DK, GLA_DV), F32)],
        scratch_shapes=[pltpu.VMEM((GLA_HEADS, GLA_DV, GLA_DK), F32)],
        compiler_params=_cparams(("parallel", "arbitrary")),
        name="gla",
    )(*args)


def _gelu_tanh(x):
    return 0.5 * x * (1.0 + jnp.tanh(0.7978845608028654 * (x + 0.044715 * x * x * x)))


def _compress_body(*refs, n_src, rows_per_src, n_groups, n_prefetch=0):
    refs = refs[n_prefetch:]
    srcs = (refs[:n_src], refs[n_src:2 * n_src])
    wk_ref, wv_ref, pe_ref, w1h_ref, w1l_ref, w2_ref, out_ref, p_ref = refs[2 * n_src:]
    j = pl.program_id(1)
    grp_per_src = rows_per_src // CMP_STRIDE

    def gather(src_refs):
        parts = []
        for r in src_refs:
            cols = []
            for l in range(CMP_STRIDE):
                if len(r.shape) == 3:
                    cols.append(r[0, pl.ds(l, grp_per_src, stride=CMP_STRIDE), :])
                else:
                    cols.append(r[pl.ds(l, grp_per_src, stride=CMP_STRIDE), :])
            parts.append(jnp.concatenate(cols, axis=1))
        a = parts[0] if len(parts) == 1 else jnp.concatenate(parts, axis=0)
        return a.astype(BF16)

    g_step = grp_per_src * n_src
    rows = pl.ds(pl.multiple_of(j * g_step, g_step), g_step)
    p_ref[rows, 0:512] = _dot(gather(srcs[0]), wk_ref[...])
    p_ref[rows, 512:1024] = _dot(gather(srcs[1]), wv_ref[...])

    @pl.when(j == pl.num_programs(1) - 1)
    def _():
        for s in range(2):
            p = p_ref[:, s * 512:(s + 1) * 512]
            pe = jnp.broadcast_to(pe_ref[s:s + 1, :], (8, CMP_LEN * NSA_HD))
            bias = _dot3(pe, w1h_ref[s], w1l_ref[s])[0:1, :]
            bias2 = jnp.concatenate([bias, bias], axis=1)
            h = p[:, 0:256] + pltpu.roll(p[:, 256:512], n_groups - 1, 0) + bias2
            out_ref[0, :, s * LANES:(s + 1) * LANES] = _dot(_gelu_tanh(h).astype(BF16), w2_ref[s]).astype(BF16)


def _compress_specs_tail():
    c2 = lambda b, j: (0, 0)
    c3 = lambda b, j: (0, 0, 0)
    return [pl.BlockSpec((16 * LANES, 512), c2), pl.BlockSpec((16 * LANES, 512), c2),
            pl.BlockSpec((2, CMP_LEN * NSA_HD), c2),
            pl.BlockSpec((2, CMP_LEN * NSA_HD, CMP_HIDDEN), c3),
            pl.BlockSpec((2, CMP_LEN * NSA_HD, CMP_HIDDEN), c3),
            pl.BlockSpec((2, 2 * CMP_HIDDEN, LANES), c3)]


def _compress_prompt(rows4, cw, nb, t_len):
    chunk = min(t_len, 2048)
    nj = t_len // chunk
    n_groups = t_len // CMP_STRIDE
    return pl.pallas_call(
        functools.partial(_compress_body, n_src=1, rows_per_src=chunk, n_groups=n_groups),
        grid=(nb, nj),
        in_specs=[pl.BlockSpec((chunk, LANES), lambda b, j: (b * nj + j, 0)),
                  pl.BlockSpec((chunk, LANES), lambda b, j: (b * nj + j, 1))] + _compress_specs_tail(),
        out_specs=pl.BlockSpec((1, n_groups, 256), lambda b, j: (b, 0, 0)),
        out_shape=jax.ShapeDtypeStruct((nb, n_groups, 256), BF16),
        scratch_shapes=[pltpu.VMEM((n_groups, 1024), F32)],
        compiler_params=_cparams(("parallel", "arbitrary")),
        name="compress_prompt",
    )(rows4, rows4, *cw)


def _compress_sample(cache3, page_table, cw, nb, n_pages):
    pages_per_step = min(n_pages, 16)
    nj = n_pages // pages_per_step
    n_groups = n_pages * PAGE_SIZE // CMP_STRIDE

    def page_spec(k, lane_blk):
        return pl.BlockSpec((1, PAGE_SIZE, LANES),
                            lambda b, j, pt: (pt[b, j * pages_per_step + k], 0, lane_blk))

    tail = _compress_specs_tail()
    tail = [pl.BlockSpec(s.block_shape, (lambda f: (lambda b, j, pt: f(b, j)))(s.index_map)) for s in tail]
    grid_spec = pltpu.PrefetchScalarGridSpec(
        num_scalar_prefetch=1,
        grid=(nb, nj),
        in_specs=[page_spec(k, 0) for k in range(pages_per_step)]
        + [page_spec(k, 1) for k in range(pages_per_step)] + tail,
        out_specs=pl.BlockSpec((1, n_groups, 256), lambda b, j, pt: (b, 0, 0)),
        scratch_shapes=[pltpu.VMEM((n_groups, 1024), F32)],
    )
    return pl.pallas_call(
        functools.partial(_compress_body, n_src=pages_per_step, rows_per_src=PAGE_SIZE, n_groups=n_groups,
                          n_prefetch=1),
        grid_spec=grid_spec,
        out_shape=jax.ShapeDtypeStruct((nb, n_groups, 256), BF16),
        compiler_params=_cparams(("parallel", "arbitrary")),
        name="compress_sample",
    )(page_table, *([cache3] * (2 * pages_per_step)), *cw)


def _rep_heads(x):
    return jnp.concatenate([x] * NSA_HPG, axis=0)


def _stack_heads(nq, g):
    parts = [nq[:, (g * NSA_HPG + h) * NSA_HD:(g * NSA_HPG + h + 1) * NSA_HD] for h in range(NSA_HPG)]
    return (jnp.concatenate(parts, axis=0) * (NSA_HD ** -0.5)).astype(BF16)


def _compressed_branch(qg, kc, vc, pos_c, qb):
    nc = kc.shape[0]
    s = _dot_nt(qg, kc)
    cmp_end = lax.broadcasted_iota(jnp.int32, (qb, nc), 1) * CMP_STRIDE + (CMP_LEN - 1)
    valid = _rep_heads(jnp.where(cmp_end <= pos_c, 1.0, 0.0))
    s = jnp.where(valid > 0.5, s, NEG_BIG)
    e = jnp.exp(s - jnp.max(s, axis=-1, keepdims=True)) * valid
    p = e / jnp.maximum(jnp.sum(e, axis=-1, keepdims=True), 1e-30)
    o_c = _dot(p.astype(BF16), vc)
    p_sum = p[0:qb]
    for h in range(1, NSA_HPG):
        p_sum = p_sum + p[h * qb:(h + 1) * qb]
    return o_c, p_sum


def _select_blocks(imp_t, pos_r, n_pick, n_blocks):
    shape = imp_t.shape
    s_io = lax.broadcasted_iota(jnp.int32, shape, 0)
    s_f = s_io.astype(F32)
    cur = jnp.right_shift(pos_r, 6)
    forced = jnp.where(s_io == 0, 1.0, 0.0) + jnp.where(s_io == cur, 1.0, 0.0) + jnp.where(s_io == cur - 1, 1.0, 0.0)
    key = jnp.where(forced > 0.5, FORCE_SCORE, imp_t)
    key = jnp.where((s_io <= cur) & (s_io < n_blocks), key, -1.0)
    sel = jnp.zeros(shape, F32)
    for _ in range(n_pick):
        m = jnp.max(key, axis=0, keepdims=True)
        cand = jnp.where(key == m, s_f, 1e6)
        idx = jnp.min(cand, axis=0, keepdims=True)
        pick = jnp.where(s_f == idx, 1.0, 0.0) * jnp.where(m >= 0.0, 1.0, 0.0)
        sel = jnp.maximum(sel, pick)
        key = jnp.where(pick > 0.5, -1.0, key)
    return sel


def _block_selection(p_sum, agg_t, pos_r, n_pick):
    hi, lo = _split(p_sum)
    imp_t = _dot_nt(agg_t, hi) + _dot_nt(agg_t, lo)
    n_blocks = p_sum.shape[1] * CMP_STRIDE // SLC_LEN
    sel_t = _select_blocks(imp_t, pos_r, n_pick, n_blocks)
    return sel_t.T.astype(BF16)


def _tile_bias(sel, kt, pos_c, qb, causal):
    s_io = lax.broadcasted_iota(jnp.int32, (N_SLC_PAD, KV_TILE), 0)
    j_io = lax.broadcasted_iota(jnp.int32, (N_SLC_PAD, KV_TILE), 1)
    expand = jnp.where(s_io == kt * BLK_PER_TILE + jnp.right_shift(j_io, 6), 1.0, 0.0).astype(BF16)
    hit = _dot(sel, expand)[0:qb]
    ok = hit > 0.5
    if causal:
        key_pos = kt * KV_TILE + lax.broadcasted_iota(jnp.int32, (qb, KV_TILE), 1)
        ok = jnp.logical_and(ok, key_pos <= pos_c)
    return _rep_heads(jnp.where(ok, 0.0, NEG_BIG))


def _flash_update(qg, k, v, bias, m, l, acc):
    s = _dot_nt(qg, k) + bias
    m_new = jnp.maximum(m, jnp.max(s, axis=-1, keepdims=True))
    alpha = jnp.exp(m - m_new)
    p = jnp.exp(s - m_new)
    l_new = alpha * l + jnp.sum(p, axis=-1, keepdims=True)
    acc_new = alpha * acc + _dot(p.astype(BF16), v)
    return m_new, l_new, acc_new


def _window_branch(qg, kw, vw, pos_w, pos_c, n_valid):
    w_len = kw.shape[0]
    lane = lax.broadcasted_iota(jnp.int32, (1, w_len), 1)
    ok = (pos_w <= pos_c) & (pos_c - pos_w < WINDOW) & (pos_w >= 0) & (lane < n_valid)
    ok_f = _rep_heads(jnp.where(ok, 1.0, 0.0))
    s = jnp.where(ok_f > 0.5, _dot_nt(qg, kw), NEG_BIG)
    e = jnp.exp(s - jnp.max(s, axis=-1, keepdims=True)) * ok_f
    p = e / jnp.maximum(jnp.sum(e, axis=-1, keepdims=True), 1e-30)
    return _dot(p.astype(BF16), vw)


def _combine_and_store(o_ref, g, gsig, o_c, o_s, o_w, nn, qb):
    def gcol(j):
        cols = [gsig[:, SM_GATE_OFF + (g * NSA_HPG + h) * 3 + j:SM_GATE_OFF + (g * NSA_HPG + h) * 3 + j + 1]
                for h in range(NSA_HPG)]
        return jnp.concatenate(cols, axis=0)

    o = gcol(0) * o_c + gcol(1) * o_s + gcol(2) * o_w
    o = o * lax.rsqrt(jnp.mean(o * o, axis=-1, keepdims=True) + RMS_EPS) * nn
    for h in range(NSA_HPG):
        c0 = (g * NSA_HPG + h) * NSA_HD
        o_ref[:, c0:c0 + NSA_HD] = o[h * qb:(h + 1) * qb]


def _nsa_prompt_body(nq_ref, sm_ref, kcv_ref, kvb_ref, agg_ref, gb_ref, nn_ref, o_ref, *, qb):
    i = pl.program_id(1)
    s0 = i * qb
    pos_c = s0 + lax.broadcasted_iota(jnp.int32, (qb, 1), 0)
    pos_r = s0 + lax.broadcasted_iota(jnp.int32, (1, qb), 1)
    gsig = _sigmoid(sm_ref[...] + gb_ref[...])
    nq = nq_ref[...]
    n_tiles = (s0 + qb - 1) // KV_TILE + 1
    w_len = WINDOW + qb
    w_start = pl.multiple_of(jnp.maximum(s0 - WINDOW, 0), qb)
    pos_w = w_start + lax.broadcasted_iota(jnp.int32, (1, w_len), 1)
    rows4 = NSA_HPG * qb

    for g in range(NSA_GROUPS):
        qg = _stack_heads(nq, g)
        kc = kcv_ref[0, :, g * NSA_HD:(g + 1) * NSA_HD]
        vc = kcv_ref[0, :, LANES + g * NSA_HD:LANES + (g + 1) * NSA_HD]
        o_c, p_sum = _compressed_branch(qg, kc, vc, pos_c, qb)
        sel = _block_selection(p_sum, agg_ref[...], pos_r, SLC_TOP)

        def tile(kt, carry):
            rows = pl.ds(pl.multiple_of(kt * KV_TILE, KV_TILE), KV_TILE)
            k = kvb_ref[0, rows, g * NSA_HD:(g + 1) * NSA_HD]
            v = kvb_ref[0, rows, LANES + g * NSA_HD:LANES + (g + 1) * NSA_HD]
            bias = _tile_bias(sel, kt, pos_c, qb, True)
            return _flash_update(qg, k, v, bias, *carry)

        init = (jnp.full((rows4, 1), NEG_BIG, F32), jnp.zeros((rows4, 1), F32), jnp.zeros((rows4, NSA_HD), F32))
        _, l, acc = lax.fori_loop(0, n_tiles, tile, init)
        o_s = acc / jnp.maximum(l, 1e-30)

        kw = kvb_ref[0, pl.ds(w_start, w_len), 2 * LANES + g * NSA_HD:2 * LANES + (g + 1) * NSA_HD]
        vw = kvb_ref[0, pl.ds(w_start, w_len), 3 * LANES + g * NSA_HD:3 * LANES + (g + 1) * NSA_HD]
        o_w = _window_branch(qg, kw, vw, pos_w, pos_c, w_len)
        _combine_and_store(o_ref, g, gsig, o_c, o_s, o_w, nn_ref[...], qb)


def _nsa_prompt(nq, sm, kcv, kvb3, agg_t, gate_b, nsa_norm, nb, t_len):
    qb = 128
    nqb = t_len // qb
    nc = t_len // CMP_STRIDE
    row = lambda b, i: (b * nqb + i, 0)
    return pl.pallas_call(
        functools.partial(_nsa_prompt_body, qb=qb),
        grid=(nb, nqb),
        in_specs=[pl.BlockSpec((qb, 512), row), pl.BlockSpec((qb, LANES), row),
                  pl.BlockSpec((1, nc, 256), lambda b, i: (b, 0, 0)),
                  pl.BlockSpec((1, t_len, 512), lambda b, i: (b, 0, 0)),
                  pl.BlockSpec((N_SLC_PAD, nc), lambda b, i: (0, 0)),
                  pl.BlockSpec((1, LANES), lambda b, i: (0, 0)),
                  pl.BlockSpec((1, NSA_HD), lambda b, i: (0, 0))],
        out_specs=pl.BlockSpec((qb, 512), row),
        out_shape=jax.ShapeDtypeStruct((nb * t_len, 512), F32),
        compiler_params=_cparams(("parallel", "arbitrary")),
        name="nsa_prompt",
    )(nq, sm, kcv, kvb3, agg_t, gate_b, nsa_norm)


def _nsa_sample_body(*refs, n_new, past_len, pages_per_tile, w_buf):
    pt_ref = refs[0]
    del pt_ref
    pages = refs[1:1 + pages_per_tile]
    (nq_ref, sm_ref, kcv_ref, new_ref, win_ref, agg_ref, gb_ref, nn_ref, o_ref,
     sel_ref, oc_ref, m_ref, l_ref, acc_ref) = refs[1 + pages_per_tile:]
    kt = pl.program_id(1)
    qb = n_new
    rows4 = NSA_HPG * qb
    pos_c = past_len + lax.broadcasted_iota(jnp.int32, (qb, 1), 0)
    nq = nq_ref[...]

    @pl.when(kt == 0)
    def _():
        pos_r = past_len + lax.broadcasted_iota(jnp.int32, (1, N_SLC_PAD), 1)
        for g in range(NSA_GROUPS):
            qg = _stack_heads(nq, g)
            kc = kcv_ref[0, :, g * NSA_HD:(g + 1) * NSA_HD]
            vc = kcv_ref[0, :, LANES + g * NSA_HD:LANES + (g + 1) * NSA_HD]
            o_c, p_sum = _compressed_branch(qg, kc, vc, pos_c, qb)
            p_pad = jnp.concatenate([p_sum, jnp.zeros((N_SLC_PAD - qb, p_sum.shape[1]), F32)], axis=0)
            sel_ref[g] = _block_selection(p_pad, agg_ref[...], pos_r, SLC_TOP - 1)
            oc_ref[g] = o_c
            m_ref[g] = jnp.full((rows4, 1), NEG_BIG, F32)
            l_ref[g] = jnp.zeros((rows4, 1), F32)
            acc_ref[g] = jnp.zeros((rows4, NSA_HD), F32)

    for g in range(NSA_GROUPS):
        qg = _stack_heads(nq, g)
        k = jnp.concatenate([p[0, :, g * NSA_HD:(g + 1) * NSA_HD] for p in pages], axis=0).astype(BF16)
        v = jnp.concatenate([p[0, :, LANES + g * NSA_HD:LANES + (g + 1) * NSA_HD] for p in pages],
                            axis=0).astype(BF16)
        bias = _tile_bias(sel_ref[g], kt, pos_c, qb, False)
        m, l, acc = _flash_update(qg, k, v, bias, m_ref[g], l_ref[g], acc_ref[g])
        m_ref[g] = m
        l_ref[g] = l
        acc_ref[g] = acc

    @pl.when(kt == pl.num_programs(1) - 1)
    def _():
        gsig = _sigmoid(sm_ref[...] + gb_ref[...])
        pad_rows = LANES - n_new
        lane = lax.broadcasted_iota(jnp.int32, (qb, LANES), 1)
        tail_ok = (lane < n_new) & (past_len + lane <= pos_c)
        tail_bias = _rep_heads(jnp.where(tail_ok, 0.0, NEG_BIG))
        w_len = w_buf + n_new
        w_pad = (-w_len) % LANES
        pos_w = (past_len - w_buf) + lax.broadcasted_iota(jnp.int32, (1, w_len + w_pad), 1)
        for g in range(NSA_GROUPS):
            qg = _stack_heads(nq, g)
            new = new_ref[...]
            k_new = jnp.concatenate([new[:, 2 * LANES + g * NSA_HD:2 * LANES + (g + 1) * NSA_HD],
                                     jnp.zeros((pad_rows, NSA_HD), F32)], axis=0).astype(BF16)
            v_new = jnp.concatenate([new[:, 3 * LANES + g * NSA_HD:3 * LANES + (g + 1) * NSA_HD],
                                     jnp.zeros((pad_rows, NSA_HD), F32)], axis=0).astype(BF16)
            _, l, acc = _flash_update(qg, k_new, v_new, tail_bias, m_ref[g], l_ref[g], acc_ref[g])
            o_s = acc / jnp.maximum(l, 1e-30)
            kw = jnp.concatenate([win_ref[0, :, g * NSA_HD:(g + 1) * NSA_HD],
                                  jnp.zeros((w_pad, NSA_HD), F32)], axis=0).astype(BF16)
            vw = jnp.concatenate([win_ref[0, :, LANES + g * NSA_HD:LANES + (g + 1) * NSA_HD],
                                  jnp.zeros((w_pad, NSA_HD), F32)], axis=0).astype(BF16)
            o_w = _window_branch(qg, kw, vw, pos_w, pos_c, w_len)
            _combine_and_store(o_ref, g, gsig, oc_ref[g], o_s, o_w, nn_ref[...], qb)


def _nsa_sample(nq, sm, kcv, rows_new, win_s, cache3, page_table, agg_t, gate_b, nsa_norm, nb, n_new, n_pages):
    past_len = n_pages * PAGE_SIZE
    pages_per_tile = KV_TILE // PAGE_SIZE
    n_tiles = past_len // KV_TILE
    nc = past_len // CMP_STRIDE
    w_len = win_s.shape[1]
    rows4 = NSA_HPG * n_new

    def page_spec(k):
        return pl.BlockSpec((1, PAGE_SIZE, 256), lambda b, t, pt: (pt[b, t * pages_per_tile + k], 0, 1))

    row = lambda b, t, pt: (b, 0)
    grid_spec = pltpu.PrefetchScalarGridSpec(
        num_scalar_prefetch=1,
        grid=(nb, n_tiles),
        in_specs=[page_spec(k) for k in range(pages_per_tile)] + [
            pl.BlockSpec((n_new, 512), row), pl.BlockSpec((n_new, LANES), row),
            pl.BlockSpec((1, nc, 256), lambda b, t, pt: (b, 0, 0)),
            pl.BlockSpec((n_new, 512), row),
            pl.BlockSpec((1, w_len, 256), lambda b, t, pt: (b, 0, 0)),
            pl.BlockSpec((N_SLC_PAD, nc), lambda b, t, pt: (0, 0)),
            pl.BlockSpec((1, LANES), lambda b, t, pt: (0, 0)),
            pl.BlockSpec((1, NSA_HD), lambda b, t, pt: (0, 0))],
        out_specs=pl.BlockSpec((n_new, 512), row),
        scratch_shapes=[pltpu.VMEM((NSA_GROUPS, N_SLC_PAD, N_SLC_PAD), BF16),
                        pltpu.VMEM((NSA_GROUPS, rows4, NSA_HD), F32),
                        pltpu.VMEM((NSA_GROUPS, rows4, 1), F32),
                        pltpu.VMEM((NSA_GROUPS, rows4, 1), F32),
                        pltpu.VMEM((NSA_GROUPS, rows4, NSA_HD), F32)],
    )
    return pl.pallas_call(
        functools.partial(_nsa_sample_body, n_new=n_new, past_len=past_len, pages_per_tile=pages_per_tile,
                          w_buf=w_len - n_new),
        grid_spec=grid_spec,
        out_shape=jax.ShapeDtypeStruct((nb * n_new, 512), F32),
        compiler_params=_cparams(("parallel", "arbitrary")),
        name="nsa_sample",
    )(page_table, *([cache3] * pages_per_tile), nq, sm, kcv, rows_new, win_s, agg_t, gate_b, nsa_norm)


def _route(logits):
    lane = lax.broadcasted_iota(jnp.int32, logits.shape, 1)
    lane_f = lane.astype(F32)
    big = 1e6
    is_g = (lane >= N_EXPERTS) & (lane < N_EXPERTS + N_EXPERT_GROUPS)
    gl = jnp.where(is_g, logits, NEG_BIG)
    gmax = jnp.max(gl, axis=-1, keepdims=True)
    gtop = jnp.min(jnp.where(gl == gmax, lane_f, big), axis=-1, keepdims=True) - N_EXPERTS
    p_group = 1.0 / jnp.sum(jnp.exp(gl - gmax), axis=-1, keepdims=True)
    in_grp = (lane < N_EXPERTS) & (jnp.right_shift(lane, 3).astype(F32) == gtop)
    el = jnp.where(in_grp, logits, NEG_BIG)
    m1 = jnp.max(el, axis=-1, keepdims=True)
    i1 = jnp.min(jnp.where(el == m1, lane_f, big), axis=-1, keepdims=True)
    el2 = jnp.where(lane_f == i1, NEG_BIG, el)
    m2 = jnp.max(el2, axis=-1, keepdims=True)
    i2 = jnp.min(jnp.where(el2 == m2, lane_f, big), axis=-1, keepdims=True)
    r = jnp.exp(m2 - m1)
    t1 = 1.0 / (1.0 + r)
    t2 = r * t1
    return jnp.where(lane_f == i1, p_group * t1, 0.0) + jnp.where(lane_f == i2, p_group * t2, 0.0)


def _mix_body(og_ref, on_ref, x_ref, wo_ref, g_ref, b_ref, rwh_ref, rwl_ref, rb_ref, x1_ref, cmb_ref):
    mix = _dot(og_ref[...].astype(BF16), wo_ref[0:512, :]) + _dot(on_ref[...].astype(BF16), wo_ref[512:1024, :])
    x1 = _layer_norm(DEEPNORM_ALPHA * x_ref[...] + mix, g_ref[...], b_ref[...])
    x1_ref[...] = x1
    logits = _dot3(x1, rwh_ref[...], rwl_ref[...]) + rb_ref[...]
    cmb_ref[...] = _route(logits)


def _mix(o_gla, o_nsa, x2d, wo, g, b, rw_hi, rw_lo, rb):
    n = x2d.shape[0]
    tm = min(n, 512)
    row = lambda i: (i, 0)
    const = lambda i: (0, 0)
    return pl.pallas_call(
        _mix_body,
        grid=(n // tm,),
        in_specs=[pl.BlockSpec((tm, 512), row), pl.BlockSpec((tm, 512), row), pl.BlockSpec((tm, D_MODEL), row),
                  pl.BlockSpec((D_MODEL, D_MODEL), const), pl.BlockSpec((1, D_MODEL), const),
                  pl.BlockSpec((1, D_MODEL), const), pl.BlockSpec((D_MODEL, LANES), const),
                  pl.BlockSpec((D_MODEL, LANES), const), pl.BlockSpec((1, LANES), const)],
        out_specs=[pl.BlockSpec((tm, D_MODEL), row), pl.BlockSpec((tm, LANES), row)],
        out_shape=[jax.ShapeDtypeStruct((n, D_MODEL), F32), jax.ShapeDtypeStruct((n, LANES), F32)],
        compiler_params=_cparams(("parallel",)),
        name="mix",
    )(o_gla, o_nsa, x2d, wo, g, b, rw_hi, rw_lo, rb)


def _moe_body(x1_ref, cmb_ref, wg_ref, wu_ref, wd_ref, g_ref, b_ref, o_ref, acc_ref, xb_ref):
    e = pl.program_id(1)

    @pl.when(e == 0)
    def _():
        acc_ref[...] = jnp.zeros_like(acc_ref)
        xb_ref[...] = x1_ref[...].astype(BF16)

    xb = xb_ref[...]
    hdn = _silu(_dot(xb, wg_ref[0])) * _dot(xb, wu_ref[0])
    y = _dot(hdn.astype(BF16), wd_ref[0])
    lane = lax.broadcasted_iota(jnp.int32, cmb_ref.shape, 1)
    w = jnp.sum(jnp.where(lane == e, cmb_ref[...], 0.0), axis=-1, keepdims=True)
    acc_ref[...] += w * y

    @pl.when(e == pl.num_programs(1) - 1)
    def _():
        o_ref[...] = _layer_norm(DEEPNORM_ALPHA * x1_ref[...] + acc_ref[...], g_ref[...], b_ref[...])


def _moe(x1, cmb, wg, wu, wd, g, b):
    n = x1.shape[0]
    tm = min(n, 512)
    row = lambda i, e: (i, 0)
    const = lambda i, e: (0, 0)
    return pl.pallas_call(
        _moe_body,
        grid=(n // tm, N_EXPERTS),
        in_specs=[pl.BlockSpec((tm, D_MODEL), row), pl.BlockSpec((tm, LANES), row),
                  pl.BlockSpec((1, D_MODEL, D_EXPERT), lambda i, e: (e, 0, 0)),
                  pl.BlockSpec((1, D_MODEL, D_EXPERT), lambda i, e: (e, 0, 0)),
                  pl.BlockSpec((1, D_EXPERT, D_MODEL), lambda i, e: (e, 0, 0)),
                  pl.BlockSpec((1, D_MODEL), const), pl.BlockSpec((1, D_MODEL), const)],
        out_specs=pl.BlockSpec((tm, D_MODEL), row),
        out_shape=jax.ShapeDtypeStruct((n, D_MODEL), F32),
        scratch_shapes=[pltpu.VMEM((tm, D_MODEL), F32), pltpu.VMEM((tm, D_MODEL), BF16)],
        compiler_params=_cparams(("parallel", "arbitrary")),
        name="moe",
    )(x1, cmb, wg, wu, wd, g, b)


def _prep_weights(w_in, gla_w_a, gla_b_a, nsa_gate_b, cmp_pe, cmp_w1, cmp_w2, gla_norm, nsa_norm, w_o,
                  ln1_g, ln1_b, router_g_w, router_g_b, router_e_w, router_e_b, w_gate, w_up, w_down,
                  ln2_g, ln2_b, n_cmp):
    o_glr, o_gog, o_nq, o_nkv, o_ngt = 1024, 1040, 1552, 2064, 2832
    pad = jnp.zeros((D_MODEL, LANES - GLA_RANK - 3 * NSA_HEADS), F32)
    w_proj = jnp.concatenate([w_in[:, 0:512], w_in[:, 512:1024], w_in[:, o_gog:o_gog + 512],
                              w_in[:, o_nq:o_nq + 512], w_in[:, o_nkv:o_nkv + 768],
                              w_in[:, o_glr:o_glr + GLA_RANK], w_in[:, o_ngt:o_ngt + 24], pad], axis=1).astype(BF16)
    wa = jnp.zeros((LANES, GLA_HEADS * GLA_DK), F32).at[0:GLA_RANK].set(gla_w_a)
    wa_hi, wa_lo = _split(wa)
    ba = gla_b_a.reshape(1, -1)
    gate_b = jnp.zeros((1, LANES), F32).at[0, SM_GATE_OFF:SM_GATE_OFF + 24].set(nsa_gate_b)
    eye = jnp.eye(NSA_GROUPS, dtype=F32)
    w1r = cmp_w1.reshape(2, CMP_LEN, NSA_HD, CMP_HIDDEN)

    def split_w1(w):
        wa_ = jnp.einsum('ldj,gh->lgdhj', w[:CMP_STRIDE], eye).reshape(CMP_STRIDE * LANES, 2 * CMP_HIDDEN)
        wb_ = jnp.einsum('ldj,gh->lgdhj', w[CMP_STRIDE:], eye).reshape(CMP_STRIDE * LANES, 2 * CMP_HIDDEN)
        return jnp.concatenate([wa_, wb_], axis=1).astype(BF16)

    wk, wv = split_w1(w1r[0]), split_w1(w1r[1])
    pe = cmp_pe.reshape(2, CMP_LEN * NSA_HD)
    w1_hi, w1_lo = _split(cmp_w1)
    w2bd = jnp.einsum('sjd,gh->sgjhd', cmp_w2, eye).reshape(2, 2 * CMP_HIDDEN, LANES).astype(BF16)
    cw = (wk, wv, pe, w1_hi, w1_lo, w2bd)
    starts = np.arange(n_cmp) * CMP_STRIDE
    sel_starts = np.arange(N_SLC_PAD) * SLC_LEN
    agg_t = ((starts[None, :] < sel_starts[:, None] + SLC_LEN)
             & (starts[None, :] + CMP_LEN > sel_starts[:, None])
             & (np.arange(n_cmp)[None, :] < n_cmp - 1))
    agg_t = jnp.asarray(agg_t, BF16)
    rw = jnp.zeros((D_MODEL, LANES), F32)
    rw = rw.at[:, 0:N_EXPERTS].set(router_e_w.transpose(1, 0, 2).reshape(D_MODEL, N_EXPERTS))
    rw = rw.at[:, N_EXPERTS:N_EXPERTS + N_EXPERT_GROUPS].set(router_g_w)
    rw_hi, rw_lo = _split(rw)
    rb = jnp.zeros((1, LANES), F32).at[0, 0:N_EXPERTS].set(router_e_b.reshape(-1))
    rb = rb.at[0, N_EXPERTS:N_EXPERTS + N_EXPERT_GROUPS].set(router_g_b)
    return dict(w_proj=w_proj, wa_hi=wa_hi, wa_lo=wa_lo, ba=ba, gate_b=gate_b, cw=cw, agg_t=agg_t,
                gn=gla_norm.reshape(1, -1), nn=nsa_norm.reshape(1, -1), wo=w_o.astype(BF16),
                ln1_g=ln1_g.reshape(1, -1), ln1_b=ln1_b.reshape(1, -1), rw_hi=rw_hi, rw_lo=rw_lo, rb=rb,
                wg=w_gate.astype(BF16), wu=w_up.astype(BF16), wd=w_down.astype(BF16),
                ln2_g=ln2_g.reshape(1, -1), ln2_b=ln2_b.reshape(1, -1))


def _tail(w, o_gla, o_nsa, x2d):
    x1, cmb = _mix(o_gla, o_nsa, x2d, w['wo'], w['ln1_g'], w['ln1_b'], w['rw_hi'], w['rw_lo'], w['rb'])
    return _moe(x1, cmb, w['wg'], w['wu'], w['wd'], w['ln2_g'], w['ln2_b'])


def kernel(x_prompt, x_sample, cache_nsa, state_win, state_gla, page_table, w_in, gla_w_a, gla_b_a, nsa_gate_b,
           cmp_pe, cmp_w1, cmp_w2, gla_norm, nsa_norm, w_o, ln1_g, ln1_b, router_g_w, router_g_b, router_e_w,
           router_e_b, w_gate, w_up, w_down, ln2_g, ln2_b):
    assert w_in.shape[0] == 1, "single layer"
    bp, tp, _ = x_prompt.shape
    bs, ts, _ = x_sample.shape
    n_pages = page_table.shape[1]
    past_len = n_pages * PAGE_SIZE
    assert tp == past_len, "prompt and past share the compressed-block count"
    w = _prep_weights(w_in[0], gla_w_a[0], gla_b_a[0], nsa_gate_b[0], cmp_pe[0], cmp_w1[0], cmp_w2[0],
                      gla_norm[0], nsa_norm[0], w_o[0], ln1_g[0], ln1_b[0], router_g_w[0], router_g_b[0],
                      router_e_w[0], router_e_b[0], w_gate[0], w_up[0], w_down[0], ln2_g[0], ln2_b[0],
                      n_cmp=tp // CMP_STRIDE)

    xp2 = x_prompt.reshape(bp * tp, D_MODEL)
    qk, v, og, nq, rows4, win, kvb, sm = _proj(xp2, w['w_proj'])
    o_gla, gla_p = _gla(qk, v, og, sm, w['wa_hi'], w['wa_lo'], w['ba'], w['gn'], None, bp, tp)
    kcv = _compress_prompt(rows4, w['cw'], bp, tp)
    o_nsa = _nsa_prompt(nq, sm, kcv, kvb.reshape(bp, tp, 512), w['agg_t'], w['gate_b'], w['nn'], bp, tp)
    y_p = _tail(w, o_gla, o_nsa, xp2).reshape(bp, tp, D_MODEL)
    w_rows = min(WINDOW, tp)
    rows_p = rows4.reshape(1, bp, tp, 4, NSA_GROUPS, NSA_HD)
    win_p = win.reshape(bp, tp, 2, NSA_GROUPS, NSA_HD)[None, :, tp - w_rows:]

    xs2 = x_sample.reshape(bs * ts, D_MODEL)
    qk, v, og, nq, rows4s, win_new, _, sm = _proj(xs2, w['w_proj'])
    o_gla, gla_s = _gla(qk, v, og, sm, w['wa_hi'], w['wa_lo'], w['ba'], w['gn'], state_gla[0], bs, ts)
    cache3 = cache_nsa[0].reshape(-1, PAGE_SIZE, 4 * NSA_GROUPS * NSA_HD)
    kcv = _compress_sample(cache3, page_table, w['cw'], bs, n_pages)
    w_buf = state_win.shape[2]
    slab = jnp.concatenate([state_win[0].reshape(bs, w_buf, 256), win_new.reshape(bs, ts, 256)], axis=1)
    o_nsa = _nsa_sample(nq, sm, kcv, rows4s, slab, cache3, page_table, w['agg_t'], w['gate_b'], w['nn'],
                        bs, ts, n_pages)
    y_s = _tail(w, o_gla, o_nsa, xs2).reshape(bs, ts, D_MODEL)
    rows_s = rows4s.reshape(1, bs, ts, 4, NSA_GROUPS, NSA_HD)
    win_s = slab[:, ts:].reshape(1, bs, w_buf, 2, NSA_GROUPS, NSA_HD)

    return (y_p, y_s, rows_p, rows_s, win_p, win_s, gla_p[None], gla_s[None])
```

```python
import functools

import numpy as np
import jax
import jax.numpy as jnp
from jax import lax
from jax.experimental import pallas as pl
from jax.experimental.pallas import tpu as pltpu

F32 = jnp.float32
BF16 = jnp.bfloat16

D_MODEL = 1024
GLA_HEADS = 4
GLA_DK = 64
GLA_DV = 128
GLA_RANK = 16
GLA_TAU = 16.0
GLA_CHUNK = 64
NSA_HEADS = 8
NSA_GROUPS = 2
NSA_HPG = 4
NSA_HD = 64
CMP_LEN = 32
CMP_STRIDE = 16
CMP_HIDDEN = 128
SLC_LEN = 64
SLC_TOP = 16
WINDOW = 512
PAGE_SIZE = 128
N_EXPERT_GROUPS = 4
EXPERTS_PER_GROUP = 8
N_EXPERTS = 32
D_EXPERT = 512
DEPTH = 1
DEEPNORM_ALPHA = (2.0 * DEPTH) ** 0.25
LN_EPS = 1e-5
RMS_EPS = 1e-6
NEG_BIG = -1e30
LOG2E = 1.4426950408889634
FORCE_SCORE = 1e9

LANES = 128
KV_TILE = 512
BLK_PER_TILE = KV_TILE // SLC_LEN
N_SLC_PAD = 128
VMEM_LIMIT = 56 * 1024 * 1024

SEG_QK, SEG_V, SEG_OG, SEG_NQ, SEG_ROWS, SEG_WIN, SEG_SM = 0, 512, 1024, 1536, 2048, 2560, 2816
D_PROJ = 2944
SM_GATE_OFF = GLA_RANK


def _cparams(sem):
    return pltpu.CompilerParams(dimension_semantics=sem, vmem_limit_bytes=VMEM_LIMIT)


def _dot(a, b):
    return jnp.dot(a, b, preferred_element_type=F32)


def _dot_nt(a, b):
    return lax.dot_general(a, b, (((1,), (1,)), ((), ())), preferred_element_type=F32)


def _dot_tn(a, b):
    return lax.dot_general(a, b, (((0,), (0,)), ((), ())), preferred_element_type=F32)


def _split(x):
    hi = x.astype(BF16)
    lo = (x - hi.astype(F32)).astype(BF16)
    return hi, lo


def _dot3(a, b_hi, b_lo):
    a_hi, a_lo = _split(a)
    return _dot(a_hi, b_hi) + (_dot(a_hi, b_lo) + _dot(a_lo, b_hi))


def _sigmoid(x):
    return 1.0 / (1.0 + jnp.exp(-x))


def _silu(x):
    return x * _sigmoid(x)


def _layer_norm(y, g, b):
    mu = jnp.mean(y, axis=-1, keepdims=True)
    d = y - mu
    var = jnp.mean(d * d, axis=-1, keepdims=True)
    return d * lax.rsqrt(var + LN_EPS) * g + b


def _proj_body(x_ref, w_ref, qk_ref, v_ref, og_ref, nq_ref, rows_ref, win_ref, kvb_ref, sm_ref):
    x = x_ref[...].astype(BF16)

    def seg(off, width):
        return _dot(x, w_ref[:, off:off + width])

    qk_ref[...] = seg(SEG_QK, 512)
    v_ref[...] = seg(SEG_V, 512)
    og_ref[...] = seg(SEG_OG, 512)
    nq_ref[...] = seg(SEG_NQ, 512)
    rows = seg(SEG_ROWS, 512)
    rows_ref[...] = rows
    win = seg(SEG_WIN, 256)
    win_ref[...] = win
    kvb_ref[:, 0:256] = rows[:, 256:512].astype(BF16)
    kvb_ref[:, 256:512] = win.astype(BF16)
    sm_ref[...] = seg(SEG_SM, 128)


def _proj(x2d, w_proj):
    n = x2d.shape[0]
    tm = min(n, 512)
    widths = (512, 512, 512, 512, 512, 256, 512, 128)
    dtypes = (F32, F32, F32, F32, F32, F32, BF16, F32)
    return pl.pallas_call(
        _proj_body,
        grid=(n // tm,),
        in_specs=[pl.BlockSpec((tm, D_MODEL), lambda i: (i, 0)),
                  pl.BlockSpec((D_MODEL, D_PROJ), lambda i: (0, 0))],
        out_specs=[pl.BlockSpec((tm, w), lambda i: (i, 0)) for w in widths],
        out_shape=[jax.ShapeDtypeStruct((n, w), d) for w, d in zip(widths, dtypes)],
        compiler_params=_cparams(("parallel",)),
        name="proj",
    )(x2d, w_proj)


def _gla_body(*refs, t_real, n_chunks, has_s0):
    if has_s0:
        qk_ref, v_ref, og_ref, sm_ref, wah_ref, wal_ref, ba_ref, gn_ref, s0_ref, o_ref, sf_ref, st_ref = refs
    else:
        qk_ref, v_ref, og_ref, sm_ref, wah_ref, wal_ref, ba_ref, gn_ref, o_ref, sf_ref, st_ref = refs
        s0_ref = None
    c_len = GLA_CHUNK
    t = pl.program_id(1)

    @pl.when(t == 0)
    def _():
        for h in range(GLA_HEADS):
            if has_s0:
                st_ref[h] = s0_ref[0, h].T
            else:
                st_ref[h] = jnp.zeros((GLA_DV, GLA_DK), F32)

    r_io = lax.broadcasted_iota(jnp.int32, (c_len, c_len), 0)
    c_io = lax.broadcasted_iota(jnp.int32, (c_len, c_len), 1)
    causal = r_io >= c_io
    tri = jnp.where(causal, 1.0, 0.0).astype(BF16)
    padded = t_real < c_len

    def load(ref, rows):
        if padded:
            x = ref[...]
            return jnp.concatenate([x, jnp.zeros((c_len - t_real, x.shape[1]), x.dtype)], axis=0)
        return ref[rows, :]

    for c in range(n_chunks):
        rows = pl.ds(c * c_len, c_len)
        qk = load(qk_ref, rows)
        v = load(v_ref, rows)
        sm = load(sm_ref, rows)
        z = _dot3(sm, wah_ref[...], wal_ref[...]) + ba_ref[...]
        la = (jnp.minimum(z, 0.0) - jnp.log1p(jnp.exp(-jnp.abs(z)))) / GLA_TAU
        if padded:
            row_id = lax.broadcasted_iota(jnp.int32, la.shape, 0)
            la = jnp.where(row_id < t_real, la, 0.0)
        la_hi, la_lo = _split(la)
        cum = _dot(tri, la_hi) + _dot(tri, la_lo)
        last = cum[c_len - 1:c_len, :]
        e_q = jnp.exp(cum)
        e_k = jnp.exp(-cum)
        e_kd = jnp.exp(last - cum)
        e_l = jnp.exp(last)
        for h in range(GLA_HEADS):
            sl = slice(h * GLA_DK, (h + 1) * GLA_DK)
            qh = qk[:, sl] * (GLA_DK ** -0.5)
            kh = qk[:, GLA_HEADS * GLA_DK + h * GLA_DK:GLA_HEADS * GLA_DK + (h + 1) * GLA_DK]
            vh = v[:, h * GLA_DV:(h + 1) * GLA_DV].astype(BF16)
            q_dec = (qh * e_q[:, sl]).astype(BF16)
            k_inv = (kh * e_k[:, sl]).astype(BF16)
            k_dec = (kh * e_kd[:, sl]).astype(BF16)
            att = jnp.where(causal, _dot_nt(q_dec, k_inv), 0.0).astype(BF16)
            s_t = st_ref[h]
            o = _dot(att, vh) + _dot_nt(q_dec, s_t.astype(BF16))
            st_ref[h] = s_t * e_l[:, sl] + _dot_tn(vh, k_dec)
            gate = _silu(load(og_ref, rows)[:, h * GLA_DV:(h + 1) * GLA_DV])
            on = o * lax.rsqrt(jnp.mean(o * o, axis=-1, keepdims=True) + RMS_EPS) * gn_ref[...]
            res = on * gate
            if padded:
                o_ref[:, h * GLA_DV:(h + 1) * GLA_DV] = res[:t_real]
            else:
                o_ref[rows, h * GLA_DV:(h + 1) * GLA_DV] = res

    @pl.when(t == pl.num_programs(1) - 1)
    def _():
        for h in range(GLA_HEADS):
            sf_ref[0, h] = st_ref[h].T


def _gla(qk, v, og, sm, wa_hi, wa_lo, ba, gn, s0, nb, t_len):
    n = nb * t_len
    if t_len >= GLA_CHUNK:
        tt = min(t_len, 512)
        t_real = GLA_CHUNK
        n_chunks = tt // GLA_CHUNK
    else:
        tt = t_len
        t_real = t_len
        n_chunks = 1
    nt = t_len // tt
    row = lambda b, t: (b * nt + t, 0)
    const = lambda b, t: (0, 0)
    in_specs = [pl.BlockSpec((tt, 512), row), pl.BlockSpec((tt, 512), row), pl.BlockSpec((tt, 512), row),
                pl.BlockSpec((tt, LANES), row),
                pl.BlockSpec((LANES, 256), const), pl.BlockSpec((LANES, 256), const),
                pl.BlockSpec((1, 256), const), pl.BlockSpec((1, GLA_DV), const)]
    args = [qk, v, og, sm, wa_hi, wa_lo, ba, gn]
    if s0 is not None:
        in_specs.append(pl.BlockSpec((1, GLA_HEADS, GLA_DK, GLA_DV), lambda b, t: (b, 0, 0, 0)))
        args.append(s0)
    return pl.pallas_call(
        functools.partial(_gla_body, t_real=t_real, n_chunks=n_chunks, has_s0=s0 is not None),
        grid=(nb, nt),
        in_specs=in_specs,
        out_specs=[pl.BlockSpec((tt, 512), row),
                   pl.BlockSpec((1, GLA_HEADS, GLA_DK, GLA_DV), lambda b, t: (b, 0, 0, 0))],
        out_shape=[jax.ShapeDtypeStruct((n, 512), F32),
                   jax.ShapeDtypeStruct((nb, GLA_HEADS, GLA_DK, GLA_DV), F32)],
        scratch_shapes=[pltpu.VMEM((GLA_HEADS, GLA_DV, GLA_DK), F32)],
        compiler_params=_cparams(("parallel", "arbitrary")),
        name="gla",
    )(*args)


def _gelu_tanh(x):
    return 0.5 * x * (1.0 + jnp.tanh(0.7978845608028654 * (x + 0.044715 * x * x * x)))


def _compress_body(*refs, n_pages, rows_per_step, n_groups):
    if n_pages:
        pages = refs[1:1 + n_pages]
        wk_ref, wv_ref, pe_ref, w1h_ref, w1l_ref, w2_ref, out_ref, p_ref, rk_ref, rv_ref = refs[1 + n_pages:]
        for idx, page in enumerate(pages):
            rk_ref[idx * PAGE_SIZE:(idx + 1) * PAGE_SIZE, :] = page[0, 0:LANES, :].T
            rv_ref[idx * PAGE_SIZE:(idx + 1) * PAGE_SIZE, :] = page[0, LANES:2 * LANES, :].T
    else:
        rk_ref, rv_ref, wk_ref, wv_ref, pe_ref, w1h_ref, w1l_ref, w2_ref, out_ref, p_ref = refs
    j = pl.program_id(1)
    g_step = rows_per_step // CMP_STRIDE

    def gather(r):
        cols = [r[pl.ds(l, g_step, stride=CMP_STRIDE), :] for l in range(CMP_STRIDE)]
        return jnp.concatenate(cols, axis=1).astype(BF16)

    rows = pl.ds(pl.multiple_of(j * g_step, g_step), g_step)
    p_ref[rows, 0:512] = _dot(gather(rk_ref), wk_ref[...])
    p_ref[rows, 512:1024] = _dot(gather(rv_ref), wv_ref[...])

    @pl.when(j == pl.num_programs(1) - 1)
    def _():
        for s in range(2):
            p = p_ref[:, s * 512:(s + 1) * 512]
            pe = jnp.broadcast_to(pe_ref[s:s + 1, :], (8, CMP_LEN * NSA_HD))
            bias = _dot3(pe, w1h_ref[s], w1l_ref[s])[0:1, :]
            bias2 = jnp.concatenate([bias, bias], axis=1)
            h = p[:, 0:256] + pltpu.roll(p[:, 256:512], n_groups - 1, 0) + bias2
            out_ref[0, :, s * LANES:(s + 1) * LANES] = _dot(_gelu_tanh(h).astype(BF16), w2_ref[s]).astype(BF16)


def _compress_specs_tail():
    c2 = lambda b, j: (0, 0)
    c3 = lambda b, j: (0, 0, 0)
    return [pl.BlockSpec((16 * LANES, 512), c2), pl.BlockSpec((16 * LANES, 512), c2),
            pl.BlockSpec((2, CMP_LEN * NSA_HD), c2),
            pl.BlockSpec((2, CMP_LEN * NSA_HD, CMP_HIDDEN), c3),
            pl.BlockSpec((2, CMP_LEN * NSA_HD, CMP_HIDDEN), c3),
            pl.BlockSpec((2, 2 * CMP_HIDDEN, LANES), c3)]


def _compress_prompt(rows4, cw, nb, t_len):
    chunk = min(t_len, 2048)
    nj = t_len // chunk
    n_groups = t_len // CMP_STRIDE
    return pl.pallas_call(
        functools.partial(_compress_body, n_pages=0, rows_per_step=chunk, n_groups=n_groups),
        grid=(nb, nj),
        in_specs=[pl.BlockSpec((chunk, LANES), lambda b, j: (b * nj + j, 0)),
                  pl.BlockSpec((chunk, LANES), lambda b, j: (b * nj + j, 1))] + _compress_specs_tail(),
        out_specs=pl.BlockSpec((1, n_groups, 256), lambda b, j: (b, 0, 0)),
        out_shape=jax.ShapeDtypeStruct((nb, n_groups, 256), BF16),
        scratch_shapes=[pltpu.VMEM((n_groups, 1024), F32)],
        compiler_params=_cparams(("parallel", "arbitrary")),
        name="compress_prompt",
    )(rows4, rows4, *cw)


def _compress_sample(cache_t, page_table, cw, nb, n_pages):
    pages_per_step = min(n_pages, 16)
    nj = n_pages // pages_per_step
    n_groups = n_pages * PAGE_SIZE // CMP_STRIDE
    rows_per_step = pages_per_step * PAGE_SIZE

    def page_spec(k):
        return pl.BlockSpec((1, 2 * LANES, PAGE_SIZE), lambda b, j, pt: (pt[b, j * pages_per_step + k], 0, 0))

    tail = _compress_specs_tail()
    tail = [pl.BlockSpec(s.block_shape, (lambda f: (lambda b, j, pt: f(b, j)))(s.index_map)) for s in tail]
    grid_spec = pltpu.PrefetchScalarGridSpec(
        num_scalar_prefetch=1,
        grid=(nb, nj),
        in_specs=[page_spec(k) for k in range(pages_per_step)] + tail,
        out_specs=pl.BlockSpec((1, n_groups, 256), lambda b, j, pt: (b, 0, 0)),
        scratch_shapes=[pltpu.VMEM((n_groups, 1024), F32), pltpu.VMEM((rows_per_step, LANES), F32),
                        pltpu.VMEM((rows_per_step, LANES), F32)],
    )
    return pl.pallas_call(
        functools.partial(_compress_body, n_pages=pages_per_step, rows_per_step=rows_per_step, n_groups=n_groups),
        grid_spec=grid_spec,
        out_shape=jax.ShapeDtypeStruct((nb, n_groups, 256), BF16),
        compiler_params=_cparams(("parallel", "arbitrary")),
        name="compress_sample",
    )(page_table, *([cache_t] * pages_per_step), *cw)


def _rep_rows(x, n):
    return jnp.concatenate([x] * n, axis=0)


def _qbd(nq, qb):
    zero = jnp.zeros((NSA_HPG * qb, NSA_HD), F32)
    blocks = []
    for g in range(NSA_GROUPS):
        parts = [nq[:, (g * NSA_HPG + h) * NSA_HD:(g * NSA_HPG + h + 1) * NSA_HD] for h in range(NSA_HPG)]
        qs = jnp.concatenate(parts, axis=0) * (NSA_HD ** -0.5 * LOG2E)
        blocks.append(jnp.concatenate([qs, zero] if g == 0 else [zero, qs], axis=1))
    return jnp.concatenate(blocks, axis=0).astype(BF16)


def _compressed_branch(qbd, kc, vc, pos_c, qb):
    nc = kc.shape[0]
    cmp_end = lax.broadcasted_iota(jnp.int32, (qb, nc), 1) * CMP_STRIDE + (CMP_LEN - 1)
    valid = jnp.where(cmp_end <= pos_c, 1.0, 0.0)
    s = _dot_nt(qbd, kc) + _rep_rows(jnp.where(valid > 0.5, 0.0, NEG_BIG), NSA_HEADS)
    e = jnp.exp2(s - jnp.max(s, axis=-1, keepdims=True)) * _rep_rows(valid, NSA_HEADS)
    p = e / jnp.maximum(jnp.sum(e, axis=-1, keepdims=True), 1e-30)
    return _dot(p.astype(BF16), vc), p


def _head_sum(p, g, qb):
    r0 = g * NSA_HPG * qb
    out = p[r0:r0 + qb]
    for h in range(1, NSA_HPG):
        out = out + p[r0 + h * qb:r0 + (h + 1) * qb]
    return out


def _select_blocks(imp_t, pos_r, n_pick, n_blocks):
    shape = imp_t.shape
    s_io = lax.broadcasted_iota(jnp.int32, shape, 0)
    s_f = s_io.astype(F32)
    cur = jnp.right_shift(pos_r, 6)
    forced = jnp.where(s_io == 0, 1.0, 0.0) + jnp.where(s_io == cur, 1.0, 0.0) + jnp.where(s_io == cur - 1, 1.0, 0.0)
    key = jnp.where(forced > 0.5, FORCE_SCORE, imp_t)
    key = jnp.where((s_io <= cur) & (s_io < n_blocks), key, -1.0)
    sel = jnp.zeros(shape, F32)
    for _ in range(n_pick):
        m = jnp.max(key, axis=0, keepdims=True)
        cand = jnp.where(key == m, s_f, 1e6)
        idx = jnp.min(cand, axis=0, keepdims=True)
        pick = jnp.where(s_f == idx, 1.0, 0.0) * jnp.where(m >= 0.0, 1.0, 0.0)
        sel = jnp.maximum(sel, pick)
        key = jnp.where(pick > 0.5, -1.0, key)
    return sel


def _unselected(p_sum, agg_t, pos_r, n_pick):
    hi, lo = _split(p_sum)
    imp_t = _dot_nt(agg_t, hi) + _dot_nt(agg_t, lo)
    n_blocks = p_sum.shape[1] * CMP_STRIDE // SLC_LEN
    sel_t = _select_blocks(imp_t, pos_r, n_pick, n_blocks)
    return (1.0 - sel_t.T).astype(BF16)


def _online_update(s, v, m, l, acc, v_transposed=False):
    m_new = jnp.maximum(m, jnp.max(s, axis=-1, keepdims=True))
    alpha = jnp.exp2(m - m_new)
    p = jnp.exp2(s - m_new)
    l_new = alpha * l + jnp.sum(p, axis=-1, keepdims=True)
    pv = _dot_nt(p.astype(BF16), v) if v_transposed else _dot(p.astype(BF16), v)
    return m_new, l_new, alpha * acc + pv


def _softmax_rows(s):
    e = jnp.exp2(s - jnp.max(s, axis=-1, keepdims=True))
    return e / jnp.sum(e, axis=-1, keepdims=True)


def _combine_and_store(o_ref, gsig, o_c, o_s, o_w, nn, qb):
    rows = NSA_HPG * qb
    for g in range(NSA_GROUPS):
        def gcol(j):
            cols = [gsig[:, SM_GATE_OFF + (g * NSA_HPG + h) * 3 + j:SM_GATE_OFF + (g * NSA_HPG + h) * 3 + j + 1]
                    for h in range(NSA_HPG)]
            return jnp.concatenate(cols, axis=0)

        def blk(x):
            return x[g * rows:(g + 1) * rows, g * NSA_HD:(g + 1) * NSA_HD]

        o = gcol(0) * blk(o_c) + gcol(1) * blk(o_s) + gcol(2) * blk(o_w)
        o = o * lax.rsqrt(jnp.mean(o * o, axis=-1, keepdims=True) + RMS_EPS) * nn
        for h in range(NSA_HPG):
            c0 = (g * NSA_HPG + h) * NSA_HD
            o_ref[:, c0:c0 + NSA_HD] = o[h * qb:(h + 1) * qb]


def _nsa_prompt_body(nq_ref, sm_ref, kcv_ref, kvb_ref, ebt_ref, agg_ref, gb_ref, nn_ref, o_ref, *, qb):
    i = pl.program_id(1)
    s0 = i * qb
    pos_c = s0 + lax.broadcasted_iota(jnp.int32, (qb, 1), 0)
    pos_r = s0 + lax.broadcasted_iota(jnp.int32, (1, qb), 1)
    rows = NSA_HEADS * qb
    qbd = _qbd(nq_ref[...], qb)

    o_c, p_c = _compressed_branch(qbd, kcv_ref[0, :, 0:LANES], kcv_ref[0, :, LANES:2 * LANES], pos_c, qb)
    unsel = []
    for g in range(NSA_GROUPS):
        unsel += [_unselected(_head_sum(p_c, g, qb), agg_ref[...], pos_r, SLC_TOP)] * NSA_HPG
    lhs = jnp.concatenate([qbd, jnp.concatenate(unsel, axis=0)], axis=1)

    def tile(kt, carry, causal):
        krows = pl.ds(pl.multiple_of(kt * KV_TILE, KV_TILE), KV_TILE)
        rhs = jnp.concatenate([kvb_ref[0, krows, 0:LANES], ebt_ref[krows, :]], axis=1)
        s = _dot_nt(lhs, rhs)
        if causal:
            key_pos = kt * KV_TILE + lax.broadcasted_iota(jnp.int32, (qb, KV_TILE), 1)
            s = s + _rep_rows(jnp.where(key_pos <= pos_c, 0.0, NEG_BIG), NSA_HEADS)
        return _online_update(s, kvb_ref[0, krows, LANES:2 * LANES], *carry)

    n_tiles = (s0 + qb - 1) // KV_TILE + 1
    init = (jnp.full((rows, 1), NEG_BIG, F32), jnp.zeros((rows, 1), F32), jnp.zeros((rows, LANES), F32))
    carry = lax.fori_loop(0, n_tiles - 1, lambda kt, c: tile(kt, c, False), init)
    _, l, acc = tile(n_tiles - 1, carry, True)
    o_s = acc / l

    w_len = WINDOW + qb
    w_start = pl.multiple_of(jnp.maximum(s0 - WINDOW, 0), qb)
    wrows = pl.ds(w_start, w_len)
    pos_w = w_start + lax.broadcasted_iota(jnp.int32, (1, w_len), 1)
    ok = (pos_w <= pos_c) & (pos_c - pos_w < WINDOW)
    s_w = _dot_nt(qbd, kvb_ref[0, wrows, 2 * LANES:3 * LANES]) + _rep_rows(jnp.where(ok, 0.0, NEG_BIG), NSA_HEADS)
    o_w = _dot(_softmax_rows(s_w).astype(BF16), kvb_ref[0, wrows, 3 * LANES:4 * LANES])

    gsig = _sigmoid(sm_ref[...] + gb_ref[...])
    _combine_and_store(o_ref, gsig, o_c, o_s, o_w, nn_ref[...], qb)


def _nsa_prompt(nq, sm, kcv, kvb3, ebt, agg_t, gate_b, nsa_norm, nb, t_len):
    qb = 128
    nqb = t_len // qb
    nc = t_len // CMP_STRIDE
    row = lambda b, i: (b * nqb + i, 0)
    return pl.pallas_call(
        functools.partial(_nsa_prompt_body, qb=qb),
        grid=(nb, nqb),
        in_specs=[pl.BlockSpec((qb, 512), row), pl.BlockSpec((qb, LANES), row),
                  pl.BlockSpec((1, nc, 256), lambda b, i: (b, 0, 0)),
                  pl.BlockSpec((1, t_len, 512), lambda b, i: (b, 0, 0)),
                  pl.BlockSpec((t_len, N_SLC_PAD), lambda b, i: (0, 0)),
                  pl.BlockSpec((N_SLC_PAD, nc), lambda b, i: (0, 0)),
                  pl.BlockSpec((1, LANES), lambda b, i: (0, 0)),
                  pl.BlockSpec((1, NSA_HD), lambda b, i: (0, 0))],
        out_specs=pl.BlockSpec((qb, 512), row),
        out_shape=jax.ShapeDtypeStruct((nb * t_len, 512), F32),
        compiler_params=_cparams(("parallel", "arbitrary")),
        name="nsa_prompt",
    )(nq, sm, kcv, kvb3, ebt, agg_t, gate_b, nsa_norm)


def _nsa_sample_body(*refs, n_new, past_len, pages_per_tile, w_buf):
    pages = refs[1:1 + pages_per_tile]
    (nq_ref, sm_ref, kcv_ref, new_ref, wnew_ref, win_ref, eb_ref, agg_ref, gb_ref, nn_ref, o_ref,
     lhs_ref, oc_ref, m_ref, l_ref, acc_ref) = refs[1 + pages_per_tile:]
    kt = pl.program_id(1)
    qb = n_new
    rows = NSA_HEADS * qb
    pos_c = past_len + lax.broadcasted_iota(jnp.int32, (qb, 1), 0)

    @pl.when(kt == 0)
    def _():
        qbd = _qbd(nq_ref[...], qb)
        o_c, p_c = _compressed_branch(qbd, kcv_ref[0, :, 0:LANES], kcv_ref[0, :, LANES:2 * LANES], pos_c, qb)
        oc_ref[...] = o_c
        pos_r = past_len + lax.broadcasted_iota(jnp.int32, (1, N_SLC_PAD), 1)
        unsel = []
        for g in range(NSA_GROUPS):
            p_sum = _head_sum(p_c, g, qb)
            p_pad = jnp.concatenate([p_sum, jnp.zeros((N_SLC_PAD - qb, p_sum.shape[1]), F32)], axis=0)
            unsel += [_unselected(p_pad, agg_ref[...], pos_r, SLC_TOP - 1)[0:qb]] * NSA_HPG
        lhs_ref[...] = jnp.concatenate([qbd, jnp.concatenate(unsel, axis=0)], axis=1)
        m_ref[...] = jnp.full((rows, 1), NEG_BIG, F32)
        l_ref[...] = jnp.zeros((rows, 1), F32)
        acc_ref[...] = jnp.zeros((rows, LANES), F32)

    k_t = jnp.concatenate([p[0, 0:LANES, :] for p in pages], axis=1).astype(BF16)
    v_t = jnp.concatenate([p[0, LANES:2 * LANES, :] for p in pages], axis=1).astype(BF16)
    s = _dot(lhs_ref[...], jnp.concatenate([k_t, eb_ref[0]], axis=0))
    m, l, acc = _online_update(s, v_t, m_ref[...], l_ref[...], acc_ref[...], v_transposed=True)
    m_ref[...] = m
    l_ref[...] = l
    acc_ref[...] = acc

    @pl.when(kt == pl.num_programs(1) - 1)
    def _():
        qbd = lhs_ref[:, 0:LANES]
        pad = jnp.zeros((LANES - n_new, LANES), F32)
        q_idx = lax.broadcasted_iota(jnp.int32, (qb, LANES), 0)
        lane = lax.broadcasted_iota(jnp.int32, (qb, LANES), 1)
        new_ok = _rep_rows(jnp.where((lane < n_new) & (lane <= q_idx), 0.0, NEG_BIG), NSA_HEADS)
        new = new_ref[...]
        k_new = jnp.concatenate([new[:, 2 * LANES:3 * LANES], pad], axis=0).astype(BF16)
        v_new = jnp.concatenate([new[:, 3 * LANES:4 * LANES], pad], axis=0).astype(BF16)
        _, l_f, acc_f = _online_update(_dot_nt(qbd, k_new) + new_ok, v_new, m, l, acc)
        o_s = acc_f / l_f
        pos_w = (past_len - w_buf) + lax.broadcasted_iota(jnp.int32, (1, w_buf), 1)
        ok = (pos_w <= pos_c) & (pos_c - pos_w < WINDOW) & (pos_w >= 0)
        s_w = _dot(qbd, win_ref[0, 0:LANES, :].astype(BF16)) + _rep_rows(jnp.where(ok, 0.0, NEG_BIG), NSA_HEADS)
        wnew = wnew_ref[...]
        kw_new = jnp.concatenate([wnew[:, 0:LANES], pad], axis=0).astype(BF16)
        vw_new = jnp.concatenate([wnew[:, LANES:2 * LANES], pad], axis=0).astype(BF16)
        s_w2 = _dot_nt(qbd, kw_new) + new_ok
        p_w = _softmax_rows(jnp.concatenate([s_w, s_w2], axis=1)).astype(BF16)
        o_w = _dot_nt(p_w[:, 0:w_buf], win_ref[0, LANES:2 * LANES, :].astype(BF16)) + _dot(p_w[:, w_buf:], vw_new)
        gsig = _sigmoid(sm_ref[...] + gb_ref[...])
        _combine_and_store(o_ref, gsig, oc_ref[...], o_s, o_w, nn_ref[...], qb)


def _nsa_sample(nq, sm, kcv, rows_new, win_new, win_t, cache_t, page_table, eb, agg_t, gate_b, nsa_norm,
                nb, n_new, n_pages):
    past_len = n_pages * PAGE_SIZE
    pages_per_tile = min(n_pages, 16)
    n_tiles = n_pages // pages_per_tile
    nc = past_len // CMP_STRIDE
    w_buf = win_t.shape[2]
    rows = NSA_HEADS * n_new

    def page_spec(k):
        return pl.BlockSpec((1, 2 * LANES, PAGE_SIZE), lambda b, t, pt: (pt[b, t * pages_per_tile + k], 1, 0))

    row = lambda b, t, pt: (b, 0)
    const2 = lambda b, t, pt: (0, 0)
    grid_spec = pltpu.PrefetchScalarGridSpec(
        num_scalar_prefetch=1,
        grid=(nb, n_tiles),
        in_specs=[page_spec(k) for k in range(pages_per_tile)] + [
            pl.BlockSpec((n_new, 512), row), pl.BlockSpec((n_new, LANES), row),
            pl.BlockSpec((1, nc, 256), lambda b, t, pt: (b, 0, 0)),
            pl.BlockSpec((n_new, 512), row), pl.BlockSpec((n_new, 256), row),
            pl.BlockSpec((1, 256, w_buf), lambda b, t, pt: (b, 0, 0)),
            pl.BlockSpec((1, N_SLC_PAD, pages_per_tile * PAGE_SIZE), lambda b, t, pt: (t, 0, 0)),
            pl.BlockSpec((N_SLC_PAD, nc), const2),
            pl.BlockSpec((1, LANES), const2), pl.BlockSpec((1, NSA_HD), const2)],
        out_specs=pl.BlockSpec((n_new, 512), row),
        scratch_shapes=[pltpu.VMEM((rows, 2 * LANES), BF16), pltpu.VMEM((rows, LANES), F32),
                        pltpu.VMEM((rows, 1), F32), pltpu.VMEM((rows, 1), F32), pltpu.VMEM((rows, LANES), F32)],
    )
    return pl.pallas_call(
        functools.partial(_nsa_sample_body, n_new=n_new, past_len=past_len, pages_per_tile=pages_per_tile,
                          w_buf=w_buf),
        grid_spec=grid_spec,
        out_shape=jax.ShapeDtypeStruct((nb * n_new, 512), F32),
        compiler_params=_cparams(("parallel", "arbitrary")),
        name="nsa_sample",
    )(page_table, *([cache_t] * pages_per_tile), nq, sm, kcv, rows_new, win_new, win_t, eb, agg_t, gate_b, nsa_norm)


def _route(logits):
    lane = lax.broadcasted_iota(jnp.int32, logits.shape, 1)
    lane_f = lane.astype(F32)
    big = 1e6
    is_g = (lane >= N_EXPERTS) & (lane < N_EXPERTS + N_EXPERT_GROUPS)
    gl = jnp.where(is_g, logits, NEG_BIG)
    gmax = jnp.max(gl, axis=-1, keepdims=True)
    gtop = jnp.min(jnp.where(gl == gmax, lane_f, big), axis=-1, keepdims=True) - N_EXPERTS
    p_group = 1.0 / jnp.sum(jnp.exp(gl - gmax), axis=-1, keepdims=True)
    in_grp = (lane < N_EXPERTS) & (jnp.right_shift(lane, 3).astype(F32) == gtop)
    el = jnp.where(in_grp, logits, NEG_BIG)
    m1 = jnp.max(el, axis=-1, keepdims=True)
    i1 = jnp.min(jnp.where(el == m1, lane_f, big), axis=-1, keepdims=True)
    el2 = jnp.where(lane_f == i1, NEG_BIG, el)
    m2 = jnp.max(el2, axis=-1, keepdims=True)
    i2 = jnp.min(jnp.where(el2 == m2, lane_f, big), axis=-1, keepdims=True)
    r = jnp.exp(m2 - m1)
    t1 = 1.0 / (1.0 + r)
    t2 = r * t1
    return jnp.where(lane_f == i1, p_group * t1, 0.0) + jnp.where(lane_f == i2, p_group * t2, 0.0)


def _mix_body(og_ref, on_ref, x_ref, wo_ref, g_ref, b_ref, rwh_ref, rwl_ref, rb_ref, x1_ref, cmb_ref):
    mix = _dot(og_ref[...].astype(BF16), wo_ref[0:512, :]) + _dot(on_ref[...].astype(BF16), wo_ref[512:1024, :])
    x1 = _layer_norm(DEEPNORM_ALPHA * x_ref[...] + mix, g_ref[...], b_ref[...])
    x1_ref[...] = x1
    logits = _dot3(x1, rwh_ref[...], rwl_ref[...]) + rb_ref[...]
    cmb_ref[...] = _route(logits)


def _mix(o_gla, o_nsa, x2d, wo, g, b, rw_hi, rw_lo, rb):
    n = x2d.shape[0]
    tm = min(n, 512)
    row = lambda i: (i, 0)
    const = lambda i: (0, 0)
    return pl.pallas_call(
        _mix_body,
        grid=(n // tm,),
        in_specs=[pl.BlockSpec((tm, 512), row), pl.BlockSpec((tm, 512), row), pl.BlockSpec((tm, D_MODEL), row),
                  pl.BlockSpec((D_MODEL, D_MODEL), const), pl.BlockSpec((1, D_MODEL), const),
                  pl.BlockSpec((1, D_MODEL), const), pl.BlockSpec((D_MODEL, LANES), const),
                  pl.BlockSpec((D_MODEL, LANES), const), pl.BlockSpec((1, LANES), const)],
        out_specs=[pl.BlockSpec((tm, D_MODEL), row), pl.BlockSpec((tm, LANES), row)],
        out_shape=[jax.ShapeDtypeStruct((n, D_MODEL), F32), jax.ShapeDtypeStruct((n, LANES), F32)],
        compiler_params=_cparams(("parallel",)),
        name="mix",
    )(o_gla, o_nsa, x2d, wo, g, b, rw_hi, rw_lo, rb)


def _moe_body(x1_ref, cmb_ref, wg_ref, wu_ref, wd_ref, g_ref, b_ref, o_ref, acc_ref, xb_ref):
    e = pl.program_id(1)

    @pl.when(e == 0)
    def _():
        acc_ref[...] = jnp.zeros_like(acc_ref)
        xb_ref[...] = x1_ref[...].astype(BF16)

    xb = xb_ref[...]
    hdn = _silu(_dot(xb, wg_ref[0])) * _dot(xb, wu_ref[0])
    y = _dot(hdn.astype(BF16), wd_ref[0])
    lane = lax.broadcasted_iota(jnp.int32, cmb_ref.shape, 1)
    w = jnp.sum(jnp.where(lane == e, cmb_ref[...], 0.0), axis=-1, keepdims=True)
    acc_ref[...] += w * y

    @pl.when(e == pl.num_programs(1) - 1)
    def _():
        o_ref[...] = _layer_norm(DEEPNORM_ALPHA * x1_ref[...] + acc_ref[...], g_ref[...], b_ref[...])


def _moe(x1, cmb, wg, wu, wd, g, b):
    n = x1.shape[0]
    tm = min(n, 512)
    row = lambda i, e: (i, 0)
    const = lambda i, e: (0, 0)
    return pl.pallas_call(
        _moe_body,
        grid=(n // tm, N_EXPERTS),
        in_specs=[pl.BlockSpec((tm, D_MODEL), row), pl.BlockSpec((tm, LANES), row),
                  pl.BlockSpec((1, D_MODEL, D_EXPERT), lambda i, e: (e, 0, 0)),
                  pl.BlockSpec((1, D_MODEL, D_EXPERT), lambda i, e: (e, 0, 0)),
                  pl.BlockSpec((1, D_EXPERT, D_MODEL), lambda i, e: (e, 0, 0)),
                  pl.BlockSpec((1, D_MODEL), const), pl.BlockSpec((1, D_MODEL), const)],
        out_specs=pl.BlockSpec((tm, D_MODEL), row),
        out_shape=jax.ShapeDtypeStruct((n, D_MODEL), F32),
        scratch_shapes=[pltpu.VMEM((tm, D_MODEL), F32), pltpu.VMEM((tm, D_MODEL), BF16)],
        compiler_params=_cparams(("parallel", "arbitrary")),
        name="moe",
    )(x1, cmb, wg, wu, wd, g, b)


def _prep_weights(w_in, gla_w_a, gla_b_a, nsa_gate_b, cmp_pe, cmp_w1, cmp_w2, gla_norm, nsa_norm, w_o,
                  ln1_g, ln1_b, router_g_w, router_g_b, router_e_w, router_e_b, w_gate, w_up, w_down,
                  ln2_g, ln2_b, n_cmp):
    o_glr, o_gog, o_nq, o_nkv, o_ngt = 1024, 1040, 1552, 2064, 2832
    pad = jnp.zeros((D_MODEL, LANES - GLA_RANK - 3 * NSA_HEADS), F32)
    w_proj = jnp.concatenate([w_in[:, 0:512], w_in[:, 512:1024], w_in[:, o_gog:o_gog + 512],
                              w_in[:, o_nq:o_nq + 512], w_in[:, o_nkv:o_nkv + 768],
                              w_in[:, o_glr:o_glr + GLA_RANK], w_in[:, o_ngt:o_ngt + 24], pad], axis=1).astype(BF16)
    wa = jnp.zeros((LANES, GLA_HEADS * GLA_DK), F32).at[0:GLA_RANK].set(gla_w_a)
    wa_hi, wa_lo = _split(wa)
    ba = gla_b_a.reshape(1, -1)
    gate_b = jnp.zeros((1, LANES), F32).at[0, SM_GATE_OFF:SM_GATE_OFF + 24].set(nsa_gate_b)
    eye = jnp.eye(NSA_GROUPS, dtype=F32)
    w1r = cmp_w1.reshape(2, CMP_LEN, NSA_HD, CMP_HIDDEN)

    def split_w1(w):
        wa_ = jnp.einsum('ldj,gh->lgdhj', w[:CMP_STRIDE], eye).reshape(CMP_STRIDE * LANES, 2 * CMP_HIDDEN)
        wb_ = jnp.einsum('ldj,gh->lgdhj', w[CMP_STRIDE:], eye).reshape(CMP_STRIDE * LANES, 2 * CMP_HIDDEN)
        return jnp.concatenate([wa_, wb_], axis=1).astype(BF16)

    wk, wv = split_w1(w1r[0]), split_w1(w1r[1])
    pe = cmp_pe.reshape(2, CMP_LEN * NSA_HD)
    w1_hi, w1_lo = _split(cmp_w1)
    w2bd = jnp.einsum('sjd,gh->sgjhd', cmp_w2, eye).reshape(2, 2 * CMP_HIDDEN, LANES).astype(BF16)
    cw = (wk, wv, pe, w1_hi, w1_lo, w2bd)
    starts = np.arange(n_cmp) * CMP_STRIDE
    sel_starts = np.arange(N_SLC_PAD) * SLC_LEN
    agg_t = ((starts[None, :] < sel_starts[:, None] + SLC_LEN)
             & (starts[None, :] + CMP_LEN > sel_starts[:, None])
             & (np.arange(n_cmp)[None, :] < n_cmp - 1))
    agg_t = jnp.asarray(agg_t, BF16)
    rw = jnp.zeros((D_MODEL, LANES), F32)
    rw = rw.at[:, 0:N_EXPERTS].set(router_e_w.transpose(1, 0, 2).reshape(D_MODEL, N_EXPERTS))
    rw = rw.at[:, N_EXPERTS:N_EXPERTS + N_EXPERT_GROUPS].set(router_g_w)
    rw_hi, rw_lo = _split(rw)
    rb = jnp.zeros((1, LANES), F32).at[0, 0:N_EXPERTS].set(router_e_b.reshape(-1))
    rb = rb.at[0, N_EXPERTS:N_EXPERTS + N_EXPERT_GROUPS].set(router_g_b)
    return dict(w_proj=w_proj, wa_hi=wa_hi, wa_lo=wa_lo, ba=ba, gate_b=gate_b, cw=cw, agg_t=agg_t,
                gn=gla_norm.reshape(1, -1), nn=nsa_norm.reshape(1, -1), wo=w_o.astype(BF16),
                ln1_g=ln1_g.reshape(1, -1), ln1_b=ln1_b.reshape(1, -1), rw_hi=rw_hi, rw_lo=rw_lo, rb=rb,
                wg=w_gate.astype(BF16), wu=w_up.astype(BF16), wd=w_down.astype(BF16),
                ln2_g=ln2_g.reshape(1, -1), ln2_b=ln2_b.reshape(1, -1))


def _tail(w, o_gla, o_nsa, x2d):
    x1, cmb = _mix(o_gla, o_nsa, x2d, w['wo'], w['ln1_g'], w['ln1_b'], w['rw_hi'], w['rw_lo'], w['rb'])
    return _moe(x1, cmb, w['wg'], w['wu'], w['wd'], w['ln2_g'], w['ln2_b'])


def kernel(x_prompt, x_sample, cache_nsa, state_win, state_gla, page_table, w_in, gla_w_a, gla_b_a, nsa_gate_b,
           cmp_pe, cmp_w1, cmp_w2, gla_norm, nsa_norm, w_o, ln1_g, ln1_b, router_g_w, router_g_b, router_e_w,
           router_e_b, w_gate, w_up, w_down, ln2_g, ln2_b):
    assert w_in.shape[0] == 1, "single layer"
    bp, tp, _ = x_prompt.shape
    bs, ts, _ = x_sample.shape
    n_pages = page_table.shape[1]
    past_len = n_pages * PAGE_SIZE
    assert tp == past_len, "prompt and past share the compressed-block count"
    w = _prep_weights(w_in[0], gla_w_a[0], gla_b_a[0], nsa_gate_b[0], cmp_pe[0], cmp_w1[0], cmp_w2[0],
                      gla_norm[0], nsa_norm[0], w_o[0], ln1_g[0], ln1_b[0], router_g_w[0], router_g_b[0],
                      router_e_w[0], router_e_b[0], w_gate[0], w_up[0], w_down[0], ln2_g[0], ln2_b[0],
                      n_cmp=tp // CMP_STRIDE)

    xp2 = x_prompt.reshape(bp * tp, D_MODEL)
    qk, v, og, nq, rows4, win, kvb, sm = _proj(xp2, w['w_proj'])
    o_gla, gla_p = _gla(qk, v, og, sm, w['wa_hi'], w['wa_lo'], w['ba'], w['gn'], None, bp, tp)
    kcv = _compress_prompt(rows4, w['cw'], bp, tp)
    blk_of_key = jnp.arange(tp, dtype=jnp.int32) // SLC_LEN
    ebt = jnp.where(blk_of_key[:, None] == jnp.arange(N_SLC_PAD, dtype=jnp.int32)[None, :], NEG_BIG, 0.0).astype(BF16)
    o_nsa = _nsa_prompt(nq, sm, kcv, kvb.reshape(bp, tp, 512), ebt, w['agg_t'], w['gate_b'], w['nn'], bp, tp)
    y_p = _tail(w, o_gla, o_nsa, xp2).reshape(bp, tp, D_MODEL)
    w_rows = min(WINDOW, tp)
    rows_p = rows4.reshape(1, bp, tp, 4, NSA_GROUPS, NSA_HD)
    win_p = win.reshape(bp, tp, 2, NSA_GROUPS, NSA_HD)[None, :, tp - w_rows:]

    xs2 = x_sample.reshape(bs * ts, D_MODEL)
    qk, v, og, nq, rows4s, win_new, _, sm = _proj(xs2, w['w_proj'])
    o_gla, gla_s = _gla(qk, v, og, sm, w['wa_hi'], w['wa_lo'], w['ba'], w['gn'], state_gla[0], bs, ts)
    cache_t = cache_nsa[0].transpose(0, 2, 3, 4, 1).reshape(-1, 4 * NSA_GROUPS * NSA_HD, PAGE_SIZE)
    w_buf = state_win.shape[2]
    win_t = state_win[0].transpose(0, 2, 3, 4, 1).reshape(bs, 2 * NSA_GROUPS * NSA_HD, w_buf)
    kcv = _compress_sample(cache_t, page_table, w['cw'], bs, n_pages)
    tile_keys = min(n_pages, 16) * PAGE_SIZE
    eb = jnp.where(blk_of_key.reshape(-1, 1, tile_keys) == jnp.arange(N_SLC_PAD, dtype=jnp.int32)[None, :, None],
                   NEG_BIG, 0.0).astype(BF16)
    o_nsa = _nsa_sample(nq, sm, kcv, rows4s, win_new, win_t, cache_t, page_table, eb, w['agg_t'], w['gate_b'],
                        w['nn'], bs, ts, n_pages)
    y_s = _tail(w, o_gla, o_nsa, xs2).reshape(bs, ts, D_MODEL)
    rows_s = rows4s.reshape(1, bs, ts, 4, NSA_GROUPS, NSA_HD)
    win_s = jnp.concatenate([state_win[0][:, ts:], win_new.reshape(bs, ts, 2, NSA_GROUPS, NSA_HD)], axis=1)[None]

    return (y_p, y_s, rows_p, rows_s, win_p, win_s, gla_p[None], gla_s[None])
```

```python
import functools

import numpy as np
import jax
import jax.numpy as jnp
from jax import lax
from jax.experimental import pallas as pl
from jax.experimental.pallas import tpu as pltpu

F32 = jnp.float32
BF16 = jnp.bfloat16

D_MODEL = 1024
GLA_HEADS = 4
GLA_DK = 64
GLA_DV = 128
GLA_RANK = 16
GLA_TAU = 16.0
GLA_CHUNK = 64
NSA_HEADS = 8
NSA_GROUPS = 2
NSA_HPG = 4
NSA_HD = 64
CMP_LEN = 32
CMP_STRIDE = 16
CMP_HIDDEN = 128
SLC_LEN = 64
SLC_TOP = 16
WINDOW = 512
PAGE_SIZE = 128
N_EXPERT_GROUPS = 4
EXPERTS_PER_GROUP = 8
N_EXPERTS = 32
D_EXPERT = 512
DEPTH = 1
DEEPNORM_ALPHA = (2.0 * DEPTH) ** 0.25
LN_EPS = 1e-5
RMS_EPS = 1e-6
NEG_BIG = -1e30
LOG2E = 1.4426950408889634
FORCE_SCORE = 1e9

LANES = 128
KV_TILE = 512
BLK_PER_TILE = KV_TILE // SLC_LEN
N_SLC_PAD = 128
VMEM_LIMIT = 56 * 1024 * 1024
MOE_TILE = 1024
MOE_CHUNK = 128
MOE_PAIR = 2

SEG_QK, SEG_V, SEG_OG, SEG_NQ, SEG_ROWS, SEG_WIN, SEG_SM = 0, 512, 1024, 1536, 2048, 2560, 2816
D_PROJ = 2944
SM_GATE_OFF = GLA_RANK


def _cparams(sem):
    return pltpu.CompilerParams(dimension_semantics=sem, vmem_limit_bytes=VMEM_LIMIT)


def _dot(a, b):
    return jnp.dot(a, b, preferred_element_type=F32)


def _dot_nt(a, b):
    return lax.dot_general(a, b, (((1,), (1,)), ((), ())), preferred_element_type=F32)


def _dot_tn(a, b):
    return lax.dot_general(a, b, (((0,), (0,)), ((), ())), preferred_element_type=F32)


def _split(x):
    hi = x.astype(BF16)
    lo = (x - hi.astype(F32)).astype(BF16)
    return hi, lo


def _dot3(a, b_hi, b_lo):
    a_hi, a_lo = _split(a)
    return _dot(a_hi, b_hi) + (_dot(a_hi, b_lo) + _dot(a_lo, b_hi))


def _sigmoid(x):
    return 1.0 / (1.0 + jnp.exp(-x))


def _silu(x):
    return x * _sigmoid(x)


def _layer_norm(y, g, b):
    mu = jnp.mean(y, axis=-1, keepdims=True)
    d = y - mu
    var = jnp.mean(d * d, axis=-1, keepdims=True)
    return d * lax.rsqrt(var + LN_EPS) * g + b


def _proj_body(x_ref, w_ref, qk_ref, v_ref, og_ref, nq_ref, rows_ref, win_ref, kvb_ref, sm_ref):
    x = x_ref[...].astype(BF16)

    def seg(off, width):
        return _dot(x, w_ref[:, off:off + width])

    qk_ref[...] = seg(SEG_QK, 512)
    v_ref[...] = seg(SEG_V, 512)
    og_ref[...] = seg(SEG_OG, 512)
    nq_ref[...] = seg(SEG_NQ, 512)
    rows = seg(SEG_ROWS, 512)
    rows_ref[...] = rows
    win = seg(SEG_WIN, 256)
    win_ref[...] = win
    kvb_ref[:, 0:256] = rows[:, 256:512].astype(BF16)
    kvb_ref[:, 256:512] = win.astype(BF16)
    sm_ref[...] = seg(SEG_SM, 128)


def _proj(x2d, w_proj):
    n = x2d.shape[0]
    tm = min(n, 512)
    widths = (512, 512, 512, 512, 512, 256, 512, 128)
    dtypes = (F32, F32, F32, F32, F32, F32, BF16, F32)
    return pl.pallas_call(
        _proj_body,
        grid=(n // tm,),
        in_specs=[pl.BlockSpec((tm, D_MODEL), lambda i: (i, 0)),
                  pl.BlockSpec((D_MODEL, D_PROJ), lambda i: (0, 0))],
        out_specs=[pl.BlockSpec((tm, w), lambda i: (i, 0)) for w in widths],
        out_shape=[jax.ShapeDtypeStruct((n, w), d) for w, d in zip(widths, dtypes)],
        compiler_params=_cparams(("parallel",)),
        name="proj",
    )(x2d, w_proj)


def _gla_body(*refs, t_real, n_chunks, has_s0):
    if has_s0:
        qk_ref, v_ref, og_ref, sm_ref, wah_ref, wal_ref, ba_ref, gn_ref, s0_ref, o_ref, sf_ref, st_ref = refs
    else:
        qk_ref, v_ref, og_ref, sm_ref, wah_ref, wal_ref, ba_ref, gn_ref, o_ref, sf_ref, st_ref = refs
        s0_ref = None
    c_len = GLA_CHUNK
    t = pl.program_id(1)

    @pl.when(t == 0)
    def _():
        for h in range(GLA_HEADS):
            if has_s0:
                st_ref[h] = s0_ref[0, h].T
            else:
                st_ref[h] = jnp.zeros((GLA_DV, GLA_DK), F32)

    r_io = lax.broadcasted_iota(jnp.int32, (c_len, c_len), 0)
    c_io = lax.broadcasted_iota(jnp.int32, (c_len, c_len), 1)
    causal = r_io >= c_io
    tri = jnp.where(causal, 1.0, 0.0).astype(BF16)
    padded = t_real < c_len

    def load(ref, rows):
        if padded:
            x = ref[...]
            return jnp.concatenate([x, jnp.zeros((c_len - t_real, x.shape[1]), x.dtype)], axis=0)
        return ref[rows, :]

    for c in range(n_chunks):
        rows = pl.ds(c * c_len, c_len)
        qk = load(qk_ref, rows)
        v = load(v_ref, rows)
        sm = load(sm_ref, rows)
        z = _dot3(sm, wah_ref[...], wal_ref[...]) + ba_ref[...]
        la = (jnp.minimum(z, 0.0) - jnp.log1p(jnp.exp(-jnp.abs(z)))) / GLA_TAU
        if padded:
            row_id = lax.broadcasted_iota(jnp.int32, la.shape, 0)
            la = jnp.where(row_id < t_real, la, 0.0)
        la_hi, la_lo = _split(la)
        cum = _dot(tri, la_hi) + _dot(tri, la_lo)
        last = cum[c_len - 1:c_len, :]
        e_q = jnp.exp(cum)
        e_k = jnp.exp(-cum)
        e_kd = jnp.exp(last - cum)
        e_l = jnp.exp(last)
        for h in range(GLA_HEADS):
            sl = slice(h * GLA_DK, (h + 1) * GLA_DK)
            qh = qk[:, sl] * (GLA_DK ** -0.5)
            kh = qk[:, GLA_HEADS * GLA_DK + h * GLA_DK:GLA_HEADS * GLA_DK + (h + 1) * GLA_DK]
            vh = v[:, h * GLA_DV:(h + 1) * GLA_DV].astype(BF16)
            q_dec = (qh * e_q[:, sl]).astype(BF16)
            k_inv = (kh * e_k[:, sl]).astype(BF16)
            k_dec = (kh * e_kd[:, sl]).astype(BF16)
            att = jnp.where(causal, _dot_nt(q_dec, k_inv), 0.0).astype(BF16)
            s_t = st_ref[h]
            o = _dot(att, vh) + _dot_nt(q_dec, s_t.astype(BF16))
            st_ref[h] = s_t * e_l[:, sl] + _dot_tn(vh, k_dec)
            gate = _silu(load(og_ref, rows)[:, h * GLA_DV:(h + 1) * GLA_DV])
            on = o * lax.rsqrt(jnp.mean(o * o, axis=-1, keepdims=True) + RMS_EPS) * gn_ref[...]
            res = on * gate
            if padded:
                o_ref[:, h * GLA_DV:(h + 1) * GLA_DV] = res[:t_real]
            else:
                o_ref[rows, h * GLA_DV:(h + 1) * GLA_DV] = res

    @pl.when(t == pl.num_programs(1) - 1)
    def _():
        for h in range(GLA_HEADS):
            sf_ref[0, h] = st_ref[h].T


def _gla(qk, v, og, sm, wa_hi, wa_lo, ba, gn, s0, nb, t_len):
    n = nb * t_len
    if t_len >= GLA_CHUNK:
        tt = min(t_len, 512)
        t_real = GLA_CHUNK
        n_chunks = tt // GLA_CHUNK
    else:
        tt = t_len
        t_real = t_len
        n_chunks = 1
    nt = t_len // tt
    row = lambda b, t: (b * nt + t, 0)
    const = lambda b, t: (0, 0)
    in_specs = [pl.BlockSpec((tt, 512), row), pl.BlockSpec((tt, 512), row), pl.BlockSpec((tt, 512), row),
                pl.BlockSpec((tt, LANES), row),
                pl.BlockSpec((LANES, 256), const), pl.BlockSpec((LANES, 256), const),
                pl.BlockSpec((1, 256), const), pl.BlockSpec((1, GLA_DV), const)]
    args = [qk, v, og, sm, wa_hi, wa_lo, ba, gn]
    if s0 is not None:
        in_specs.append(pl.BlockSpec((1, GLA_HEADS, GLA_DK, GLA_DV), lambda b, t: (b, 0, 0, 0)))
        args.append(s0)
    return pl.pallas_call(
        functools.partial(_gla_body, t_real=t_real, n_chunks=n_chunks, has_s0=s0 is not None),
        grid=(nb, nt),
        in_specs=in_specs,
        out_specs=[pl.BlockSpec((tt, 512), row),
                   pl.BlockSpec((1, GLA_HEADS, GLA_DK, GLA_DV), lambda b, t: (b, 0, 0, 0))],
        out_shape=[jax.ShapeDtypeStruct((n, 512), F32),
                   jax.ShapeDtypeStruct((nb, GLA_HEADS, GLA_DK, GLA_DV), F32)],
        scratch_shapes=[pltpu.VMEM((GLA_HEADS, GLA_DV, GLA_DK), F32)],
        compiler_params=_cparams(("parallel", "arbitrary")),
        name="gla",
    )(*args)


def _gelu_tanh(x):
    return 0.5 * x * (1.0 + jnp.tanh(0.7978845608028654 * (x + 0.044715 * x * x * x)))


def _compress_body(*refs, n_pages, rows_per_step, n_groups):
    if n_pages:
        pages = refs[1:1 + n_pages]
        wk_ref, wv_ref, pe_ref, w1h_ref, w1l_ref, w2_ref, out_ref, p_ref, rk_ref, rv_ref = refs[1 + n_pages:]
        for idx, page in enumerate(pages):
            rk_ref[idx * PAGE_SIZE:(idx + 1) * PAGE_SIZE, :] = page[0, 0:LANES, :].T
            rv_ref[idx * PAGE_SIZE:(idx + 1) * PAGE_SIZE, :] = page[0, LANES:2 * LANES, :].T
    else:
        rk_ref, rv_ref, wk_ref, wv_ref, pe_ref, w1h_ref, w1l_ref, w2_ref, out_ref, p_ref = refs
    j = pl.program_id(1)
    g_step = rows_per_step // CMP_STRIDE

    def gather(r):
        cols = [r[pl.ds(l, g_step, stride=CMP_STRIDE), :] for l in range(CMP_STRIDE)]
        return jnp.concatenate(cols, axis=1).astype(BF16)

    rows = pl.ds(pl.multiple_of(j * g_step, g_step), g_step)
    p_ref[rows, 0:512] = _dot(gather(rk_ref), wk_ref[...])
    p_ref[rows, 512:1024] = _dot(gather(rv_ref), wv_ref[...])

    @pl.when(j == pl.num_programs(1) - 1)
    def _():
        for s in range(2):
            p = p_ref[:, s * 512:(s + 1) * 512]
            pe = jnp.broadcast_to(pe_ref[s:s + 1, :], (8, CMP_LEN * NSA_HD))
            bias = _dot3(pe, w1h_ref[s], w1l_ref[s])[0:1, :]
            bias2 = jnp.concatenate([bias, bias], axis=1)
            h = p[:, 0:256] + pltpu.roll(p[:, 256:512], n_groups - 1, 0) + bias2
            out_ref[0, :, s * LANES:(s + 1) * LANES] = _dot(_gelu_tanh(h).astype(BF16), w2_ref[s]).astype(BF16)


def _compress_specs_tail():
    c2 = lambda b, j: (0, 0)
    c3 = lambda b, j: (0, 0, 0)
    return [pl.BlockSpec((16 * LANES, 512), c2), pl.BlockSpec((16 * LANES, 512), c2),
            pl.BlockSpec((2, CMP_LEN * NSA_HD), c2),
            pl.BlockSpec((2, CMP_LEN * NSA_HD, CMP_HIDDEN), c3),
            pl.BlockSpec((2, CMP_LEN * NSA_HD, CMP_HIDDEN), c3),
            pl.BlockSpec((2, 2 * CMP_HIDDEN, LANES), c3)]


def _compress_prompt(rows4, cw, nb, t_len):
    chunk = min(t_len, 2048)
    nj = t_len // chunk
    n_groups = t_len // CMP_STRIDE
    return pl.pallas_call(
        functools.partial(_compress_body, n_pages=0, rows_per_step=chunk, n_groups=n_groups),
        grid=(nb, nj),
        in_specs=[pl.BlockSpec((chunk, LANES), lambda b, j: (b * nj + j, 0)),
                  pl.BlockSpec((chunk, LANES), lambda b, j: (b * nj + j, 1))] + _compress_specs_tail(),
        out_specs=pl.BlockSpec((1, n_groups, 256), lambda b, j: (b, 0, 0)),
        out_shape=jax.ShapeDtypeStruct((nb, n_groups, 256), BF16),
        scratch_shapes=[pltpu.VMEM((n_groups, 1024), F32)],
        compiler_params=_cparams(("parallel", "arbitrary")),
        name="compress_prompt",
    )(rows4, rows4, *cw)


def _compress_sample(cache_t, page_table, cw, nb, n_pages):
    pages_per_step = min(n_pages, 16)
    nj = n_pages // pages_per_step
    n_groups = n_pages * PAGE_SIZE // CMP_STRIDE
    rows_per_step = pages_per_step * PAGE_SIZE

    def page_spec(k):
        return pl.BlockSpec((1, 2 * LANES, PAGE_SIZE), lambda b, j, pt: (pt[b, j * pages_per_step + k], 0, 0))

    tail = _compress_specs_tail()
    tail = [pl.BlockSpec(s.block_shape, (lambda f: (lambda b, j, pt: f(b, j)))(s.index_map)) for s in tail]
    grid_spec = pltpu.PrefetchScalarGridSpec(
        num_scalar_prefetch=1,
        grid=(nb, nj),
        in_specs=[page_spec(k) for k in range(pages_per_step)] + tail,
        out_specs=pl.BlockSpec((1, n_groups, 256), lambda b, j, pt: (b, 0, 0)),
        scratch_shapes=[pltpu.VMEM((n_groups, 1024), F32), pltpu.VMEM((rows_per_step, LANES), F32),
                        pltpu.VMEM((rows_per_step, LANES), F32)],
    )
    return pl.pallas_call(
        functools.partial(_compress_body, n_pages=pages_per_step, rows_per_step=rows_per_step, n_groups=n_groups),
        grid_spec=grid_spec,
        out_shape=jax.ShapeDtypeStruct((nb, n_groups, 256), BF16),
        compiler_params=_cparams(("parallel", "arbitrary")),
        name="compress_sample",
    )(page_table, *([cache_t] * pages_per_step), *cw)


def _rep_rows(x, n):
    return jnp.concatenate([x] * n, axis=0)


def _qbd(nq, qb):
    zero = jnp.zeros((NSA_HPG * qb, NSA_HD), F32)
    blocks = []
    for g in range(NSA_GROUPS):
        parts = [nq[:, (g * NSA_HPG + h) * NSA_HD:(g * NSA_HPG + h + 1) * NSA_HD] for h in range(NSA_HPG)]
        qs = jnp.concatenate(parts, axis=0) * (NSA_HD ** -0.5 * LOG2E)
        blocks.append(jnp.concatenate([qs, zero] if g == 0 else [zero, qs], axis=1))
    return jnp.concatenate(blocks, axis=0).astype(BF16)


def _compressed_branch(qbd, kc, vc, pos_c, qb):
    nc = kc.shape[0]
    cmp_end = lax.broadcasted_iota(jnp.int32, (qb, nc), 1) * CMP_STRIDE + (CMP_LEN - 1)
    valid = jnp.where(cmp_end <= pos_c, 1.0, 0.0)
    s = _dot_nt(qbd, kc) + _rep_rows(jnp.where(valid > 0.5, 0.0, NEG_BIG), NSA_HEADS)
    e = jnp.exp2(s - jnp.max(s, axis=-1, keepdims=True)) * _rep_rows(valid, NSA_HEADS)
    p = e / jnp.maximum(jnp.sum(e, axis=-1, keepdims=True), 1e-30)
    return _dot(p.astype(BF16), vc), p


def _head_sum(p, g, qb):
    r0 = g * NSA_HPG * qb
    out = p[r0:r0 + qb]
    for h in range(1, NSA_HPG):
        out = out + p[r0 + h * qb:r0 + (h + 1) * qb]
    return out


def _select_blocks(imp_t, pos_r, n_pick, n_blocks):
    shape = imp_t.shape
    s_io = lax.broadcasted_iota(jnp.int32, shape, 0)
    cur = jnp.right_shift(pos_r, 6)
    valid = (s_io <= cur) & (s_io < n_blocks)
    forced = (s_io == 0) | (s_io == cur) | (s_io == cur - 1)
    bits = pltpu.bitcast(imp_t, jnp.int32)
    packed = pltpu.bitcast((bits & jnp.int32(-N_SLC_PAD)) | (N_SLC_PAD - 1 - s_io), F32)
    key = jnp.where(valid & jnp.logical_not(forced), packed, -1.0)
    sel = jnp.where(valid & forced, 1.0, 0.0)
    for _ in range(n_pick):
        m = jnp.max(key, axis=0, keepdims=True)
        pick = (key == m) & (m >= 0.0)
        sel = jnp.where(pick, 1.0, sel)
        key = jnp.where(pick, -1.0, key)
    return sel


def _unselected(p_sum, agg_t, pos_r, n_pick):
    hi, lo = _split(p_sum)
    imp_t = _dot_nt(agg_t, hi) + _dot_nt(agg_t, lo)
    n_blocks = p_sum.shape[1] * CMP_STRIDE // SLC_LEN
    sel_t = _select_blocks(imp_t, pos_r, n_pick, n_blocks)
    return (1.0 - sel_t.T).astype(BF16)


def _online_update(s, v, m, l, acc, v_transposed=False):
    m_new = jnp.maximum(m, jnp.max(s, axis=-1, keepdims=True))
    alpha = jnp.exp2(m - m_new)
    p = jnp.exp2(s - m_new)
    l_new = alpha * l + jnp.sum(p, axis=-1, keepdims=True)
    pv = _dot_nt(p.astype(BF16), v) if v_transposed else _dot(p.astype(BF16), v)
    return m_new, l_new, alpha * acc + pv


def _softmax_rows(s):
    e = jnp.exp2(s - jnp.max(s, axis=-1, keepdims=True))
    return e / jnp.sum(e, axis=-1, keepdims=True)


def _combine_and_store(o_ref, gsig, o_c, o_s, o_w, nn, qb):
    rows = NSA_HPG * qb
    for g in range(NSA_GROUPS):
        def gcol(j):
            cols = [gsig[:, SM_GATE_OFF + (g * NSA_HPG + h) * 3 + j:SM_GATE_OFF + (g * NSA_HPG + h) * 3 + j + 1]
                    for h in range(NSA_HPG)]
            return jnp.concatenate(cols, axis=0)

        def blk(x):
            return x[g * rows:(g + 1) * rows, g * NSA_HD:(g + 1) * NSA_HD]

        o = gcol(0) * blk(o_c) + gcol(1) * blk(o_s) + gcol(2) * blk(o_w)
        o = o * lax.rsqrt(jnp.mean(o * o, axis=-1, keepdims=True) + RMS_EPS) * nn
        for h in range(NSA_HPG):
            c0 = (g * NSA_HPG + h) * NSA_HD
            o_ref[:, c0:c0 + NSA_HD] = o[h * qb:(h + 1) * qb]


def _nsa_prompt_body(nq_ref, sm_ref, kcv_ref, kvb_ref, ebt_ref, agg_ref, gb_ref, nn_ref, o_ref, *, qb):
    i = pl.program_id(1)
    s0 = i * qb
    pos_c = s0 + lax.broadcasted_iota(jnp.int32, (qb, 1), 0)
    pos_r = s0 + lax.broadcasted_iota(jnp.int32, (1, qb), 1)
    rows = NSA_HEADS * qb
    qbd = _qbd(nq_ref[...], qb)

    o_c, p_c = _compressed_branch(qbd, kcv_ref[0, :, 0:LANES], kcv_ref[0, :, LANES:2 * LANES], pos_c, qb)
    unsel = []
    for g in range(NSA_GROUPS):
        unsel += [_unselected(_head_sum(p_c, g, qb), agg_ref[...], pos_r, SLC_TOP - 3)] * NSA_HPG
    lhs = jnp.concatenate([qbd, jnp.concatenate(unsel, axis=0)], axis=1)

    def tile(kt, carry, causal):
        krows = pl.ds(pl.multiple_of(kt * KV_TILE, KV_TILE), KV_TILE)
        rhs = jnp.concatenate([kvb_ref[0, krows, 0:LANES], ebt_ref[krows, :]], axis=1)
        s = _dot_nt(lhs, rhs)
        if causal:
            key_pos = kt * KV_TILE + lax.broadcasted_iota(jnp.int32, (qb, KV_TILE), 1)
            s = s + _rep_rows(jnp.where(key_pos <= pos_c, 0.0, NEG_BIG), NSA_HEADS)
        return _online_update(s, kvb_ref[0, krows, LANES:2 * LANES], *carry)

    n_tiles = (s0 + qb - 1) // KV_TILE + 1
    init = (jnp.full((rows, 1), NEG_BIG, F32), jnp.zeros((rows, 1), F32), jnp.zeros((rows, LANES), F32))
    carry = lax.fori_loop(0, n_tiles - 1, lambda kt, c: tile(kt, c, False), init)
    _, l, acc = tile(n_tiles - 1, carry, True)
    o_s = acc / l

    w_len = WINDOW + qb
    w_start = pl.multiple_of(jnp.maximum(s0 - WINDOW, 0), qb)
    wrows = pl.ds(w_start, w_len)
    pos_w = w_start + lax.broadcasted_iota(jnp.int32, (1, w_len), 1)
    ok = (pos_w <= pos_c) & (pos_c - pos_w < WINDOW)
    s_w = _dot_nt(qbd, kvb_ref[0, wrows, 2 * LANES:3 * LANES]) + _rep_rows(jnp.where(ok, 0.0, NEG_BIG), NSA_HEADS)
    o_w = _dot(_softmax_rows(s_w).astype(BF16), kvb_ref[0, wrows, 3 * LANES:4 * LANES])

    gsig = _sigmoid(sm_ref[...] + gb_ref[...])
    _combine_and_store(o_ref, gsig, o_c, o_s, o_w, nn_ref[...], qb)


def _nsa_prompt(nq, sm, kcv, kvb3, ebt, agg_t, gate_b, nsa_norm, nb, t_len):
    qb = 128
    nqb = t_len // qb
    nc = t_len // CMP_STRIDE
    row = lambda b, i: (b * nqb + i, 0)
    return pl.pallas_call(
        functools.partial(_nsa_prompt_body, qb=qb),
        grid=(nb, nqb),
        in_specs=[pl.BlockSpec((qb, 512), row), pl.BlockSpec((qb, LANES), row),
                  pl.BlockSpec((1, nc, 256), lambda b, i: (b, 0, 0)),
                  pl.BlockSpec((1, t_len, 512), lambda b, i: (b, 0, 0)),
                  pl.BlockSpec((t_len, N_SLC_PAD), lambda b, i: (0, 0)),
                  pl.BlockSpec((N_SLC_PAD, nc), lambda b, i: (0, 0)),
                  pl.BlockSpec((1, LANES), lambda b, i: (0, 0)),
                  pl.BlockSpec((1, NSA_HD), lambda b, i: (0, 0))],
        out_specs=pl.BlockSpec((qb, 512), row),
        out_shape=jax.ShapeDtypeStruct((nb * t_len, 512), F32),
        compiler_params=_cparams(("parallel", "arbitrary")),
        name="nsa_prompt",
    )(nq, sm, kcv, kvb3, ebt, agg_t, gate_b, nsa_norm)


def _nsa_sample_body(*refs, n_new, past_len, pages_per_tile, w_buf):
    pages = refs[1:1 + pages_per_tile]
    (nq_ref, sm_ref, kcv_ref, new_ref, wnew_ref, win_ref, eb_ref, agg_ref, gb_ref, nn_ref, o_ref,
     lhs_ref, oc_ref, m_ref, l_ref, acc_ref) = refs[1 + pages_per_tile:]
    kt = pl.program_id(1)
    qb = n_new
    rows = NSA_HEADS * qb
    pos_c = past_len + lax.broadcasted_iota(jnp.int32, (qb, 1), 0)

    @pl.when(kt == 0)
    def _():
        qbd = _qbd(nq_ref[...], qb)
        o_c, p_c = _compressed_branch(qbd, kcv_ref[0, :, 0:LANES], kcv_ref[0, :, LANES:2 * LANES], pos_c, qb)
        oc_ref[...] = o_c
        pos_r = past_len + lax.broadcasted_iota(jnp.int32, (1, N_SLC_PAD), 1)
        unsel = []
        for g in range(NSA_GROUPS):
            p_sum = _head_sum(p_c, g, qb)
            p_pad = jnp.concatenate([p_sum, jnp.zeros((N_SLC_PAD - qb, p_sum.shape[1]), F32)], axis=0)
            unsel += [_unselected(p_pad, agg_ref[...], pos_r, SLC_TOP - 3)[0:qb]] * NSA_HPG
        lhs_ref[...] = jnp.concatenate([qbd, jnp.concatenate(unsel, axis=0)], axis=1)
        m_ref[...] = jnp.full((rows, 1), NEG_BIG, F32)
        l_ref[...] = jnp.zeros((rows, 1), F32)
        acc_ref[...] = jnp.zeros((rows, LANES), F32)

    k_t = jnp.concatenate([p[0, 0:LANES, :] for p in pages], axis=1).astype(BF16)
    v_t = jnp.concatenate([p[0, LANES:2 * LANES, :] for p in pages], axis=1).astype(BF16)
    s = _dot(lhs_ref[...], jnp.concatenate([k_t, eb_ref[0]], axis=0))
    m, l, acc = _online_update(s, v_t, m_ref[...], l_ref[...], acc_ref[...], v_transposed=True)
    m_ref[...] = m
    l_ref[...] = l
    acc_ref[...] = acc

    @pl.when(kt == pl.num_programs(1) - 1)
    def _():
        qbd = lhs_ref[:, 0:LANES]
        pad = jnp.zeros((LANES - n_new, LANES), F32)
        q_idx = lax.broadcasted_iota(jnp.int32, (qb, LANES), 0)
        lane = lax.broadcasted_iota(jnp.int32, (qb, LANES), 1)
        new_ok = _rep_rows(jnp.where((lane < n_new) & (lane <= q_idx), 0.0, NEG_BIG), NSA_HEADS)
        new = new_ref[...]
        k_new = jnp.concatenate([new[:, 2 * LANES:3 * LANES], pad], axis=0).astype(BF16)
        v_new = jnp.concatenate([new[:, 3 * LANES:4 * LANES], pad], axis=0).astype(BF16)
        _, l_f, acc_f = _online_update(_dot_nt(qbd, k_new) + new_ok, v_new, m, l, acc)
        o_s = acc_f / l_f
        pos_w = (past_len - w_buf) + lax.broadcasted_iota(jnp.int32, (1, w_buf), 1)
        ok = (pos_w <= pos_c) & (pos_c - pos_w < WINDOW) & (pos_w >= 0)
        s_w = _dot(qbd, win_ref[0, 0:LANES, :].astype(BF16)) + _rep_rows(jnp.where(ok, 0.0, NEG_BIG), NSA_HEADS)
        wnew = wnew_ref[...]
        kw_new = jnp.concatenate([wnew[:, 0:LANES], pad], axis=0).astype(BF16)
        vw_new = jnp.concatenate([wnew[:, LANES:2 * LANES], pad], axis=0).astype(BF16)
        s_w2 = _dot_nt(qbd, kw_new) + new_ok
        p_w = _softmax_rows(jnp.concatenate([s_w, s_w2], axis=1)).astype(BF16)
        o_w = _dot_nt(p_w[:, 0:w_buf], win_ref[0, LANES:2 * LANES, :].astype(BF16)) + _dot(p_w[:, w_buf:], vw_new)
        gsig = _sigmoid(sm_ref[...] + gb_ref[...])
        _combine_and_store(o_ref, gsig, oc_ref[...], o_s, o_w, nn_ref[...], qb)


def _nsa_sample(nq, sm, kcv, rows_new, win_new, win_t, cache_t, page_table, eb, agg_t, gate_b, nsa_norm,
                nb, n_new, n_pages):
    past_len = n_pages * PAGE_SIZE
    pages_per_tile = min(n_pages, 16)
    n_tiles = n_pages // pages_per_tile
    nc = past_len // CMP_STRIDE
    w_buf = win_t.shape[2]
    rows = NSA_HEADS * n_new

    def page_spec(k):
        return pl.BlockSpec((1, 2 * LANES, PAGE_SIZE), lambda b, t, pt: (pt[b, t * pages_per_tile + k], 1, 0))

    row = lambda b, t, pt: (b, 0)
    const2 = lambda b, t, pt: (0, 0)
    grid_spec = pltpu.PrefetchScalarGridSpec(
        num_scalar_prefetch=1,
        grid=(nb, n_tiles),
        in_specs=[page_spec(k) for k in range(pages_per_tile)] + [
            pl.BlockSpec((n_new, 512), row), pl.BlockSpec((n_new, LANES), row),
            pl.BlockSpec((1, nc, 256), lambda b, t, pt: (b, 0, 0)),
            pl.BlockSpec((n_new, 512), row), pl.BlockSpec((n_new, 256), row),
            pl.BlockSpec((1, 256, w_buf), lambda b, t, pt: (b, 0, 0)),
            pl.BlockSpec((1, N_SLC_PAD, pages_per_tile * PAGE_SIZE), lambda b, t, pt: (t, 0, 0)),
            pl.BlockSpec((N_SLC_PAD, nc), const2),
            pl.BlockSpec((1, LANES), const2), pl.BlockSpec((1, NSA_HD), const2)],
        out_specs=pl.BlockSpec((n_new, 512), row),
        scratch_shapes=[pltpu.VMEM((rows, 2 * LANES), BF16), pltpu.VMEM((rows, LANES), F32),
                        pltpu.VMEM((rows, 1), F32), pltpu.VMEM((rows, 1), F32), pltpu.VMEM((rows, LANES), F32)],
    )
    return pl.pallas_call(
        functools.partial(_nsa_sample_body, n_new=n_new, past_len=past_len, pages_per_tile=pages_per_tile,
                          w_buf=w_buf),
        grid_spec=grid_spec,
        out_shape=jax.ShapeDtypeStruct((nb * n_new, 512), F32),
        compiler_params=_cparams(("parallel", "arbitrary")),
        name="nsa_sample",
    )(page_table, *([cache_t] * pages_per_tile), nq, sm, kcv, rows_new, win_new, win_t, eb, agg_t, gate_b, nsa_norm)


def _route(logits):
    lane = lax.broadcasted_iota(jnp.int32, logits.shape, 1)
    lane_f = lane.astype(F32)
    big = 1e6
    is_g = (lane >= N_EXPERTS) & (lane < N_EXPERTS + N_EXPERT_GROUPS)
    gl = jnp.where(is_g, logits, NEG_BIG)
    gmax = jnp.max(gl, axis=-1, keepdims=True)
    gtop = jnp.min(jnp.where(gl == gmax, lane_f, big), axis=-1, keepdims=True) - N_EXPERTS
    p_group = 1.0 / jnp.sum(jnp.exp(gl - gmax), axis=-1, keepdims=True)
    in_grp = (lane < N_EXPERTS) & (jnp.right_shift(lane, 3).astype(F32) == gtop)
    el = jnp.where(in_grp, logits, NEG_BIG)
    m1 = jnp.max(el, axis=-1, keepdims=True)
    i1 = jnp.min(jnp.where(el == m1, lane_f, big), axis=-1, keepdims=True)
    el2 = jnp.where(lane_f == i1, NEG_BIG, el)
    m2 = jnp.max(el2, axis=-1, keepdims=True)
    i2 = jnp.min(jnp.where(el2 == m2, lane_f, big), axis=-1, keepdims=True)
    r = jnp.exp(m2 - m1)
    t1 = 1.0 / (1.0 + r)
    t2 = r * t1
    return jnp.where(lane_f == i1, p_group * t1, 0.0) + jnp.where(lane_f == i2, p_group * t2, 0.0)


def _mix_body(og_ref, on_ref, x_ref, wo_ref, g_ref, b_ref, rwh_ref, rwl_ref, rb_ref, x1_ref, cmb_ref):
    mix = _dot(og_ref[...].astype(BF16), wo_ref[0:512, :]) + _dot(on_ref[...].astype(BF16), wo_ref[512:1024, :])
    x1 = _layer_norm(DEEPNORM_ALPHA * x_ref[...] + mix, g_ref[...], b_ref[...])
    x1_ref[...] = x1
    logits = _dot3(x1, rwh_ref[...], rwl_ref[...]) + rb_ref[...]
    cmb_ref[...] = _route(logits)


def _mix(o_gla, o_nsa, x2d, wo, g, b, rw_hi, rw_lo, rb):
    n = x2d.shape[0]
    tm = min(n, 512)
    row = lambda i: (i, 0)
    const = lambda i: (0, 0)
    return pl.pallas_call(
        _mix_body,
        grid=(n // tm,),
        in_specs=[pl.BlockSpec((tm, 512), row), pl.BlockSpec((tm, 512), row), pl.BlockSpec((tm, D_MODEL), row),
                  pl.BlockSpec((D_MODEL, D_MODEL), const), pl.BlockSpec((1, D_MODEL), const),
                  pl.BlockSpec((1, D_MODEL), const), pl.BlockSpec((D_MODEL, LANES), const),
                  pl.BlockSpec((D_MODEL, LANES), const), pl.BlockSpec((1, LANES), const)],
        out_specs=[pl.BlockSpec((tm, D_MODEL), row), pl.BlockSpec((tm, LANES), row)],
        out_shape=[jax.ShapeDtypeStruct((n, D_MODEL), F32), jax.ShapeDtypeStruct((n, LANES), F32)],
        compiler_params=_cparams(("parallel",)),
        name="mix",
    )(o_gla, o_nsa, x2d, wo, g, b, rw_hi, rw_lo, rb)


def _moe_body(x1_ref, cmb_ref, tri_ref, wg_ref, wu_ref, wd_ref, g_ref, b_ref, o_ref,
              acc_ref, xb_ref, key_ref, keyt_ref, cmbt_ref, *, chunk):
    j = pl.program_id(1)
    tm = x1_ref.shape[0]

    @pl.when(j == 0)
    def _():
        acc_ref[...] = jnp.zeros_like(acc_ref)
        xb_ref[...] = x1_ref[...].astype(BF16)
        cmb = cmb_ref[...]
        routed = cmb != 0.0
        rank = _dot(tri_ref[...], jnp.where(routed, 1.0, 0.0).astype(BF16))
        key = jnp.where(routed, rank, -1.0)
        key_ref[...] = key
        keyt_ref[...] = key.T
        cmbt_ref[...] = cmb.T

    lane = lax.broadcasted_iota(jnp.int32, (tm, LANES), 1)
    r_io = lax.broadcasted_iota(jnp.int32, (chunk, tm), 0).astype(F32)
    c_io = lax.broadcasted_iota(jnp.int32, (tm, chunk), 1).astype(F32)
    key_rows, w_rows, key_cols, counts = [], [], [], []
    for k in range(MOE_PAIR):
        e = MOE_PAIR * j + k
        key_rows.append(keyt_ref[pl.ds(e, 1), :])
        w_rows.append(cmbt_ref[pl.ds(e, 1), :])
        key_cols.append(jnp.sum(jnp.where(lane == e, key_ref[...], 0.0), axis=-1, keepdims=True))
        counts.append(jnp.sum(jnp.where(key_rows[k] >= 0.0, 1.0, 0.0)).astype(jnp.int32))

    def one_chunk(c, carry):
        base = (c * chunk).astype(F32)
        sels = [jnp.where(key_rows[k] - base == r_io, 1.0, 0.0) for k in range(MOE_PAIR)]
        xe = _dot(jnp.concatenate(sels, axis=0).astype(BF16), xb_ref[...]).astype(BF16)
        ys = []
        for k in range(MOE_PAIR):
            xk = xe[k * chunk:(k + 1) * chunk]
            hdn = _silu(_dot(xk, wg_ref[k])) * _dot(xk, wu_ref[k])
            wc = jnp.sum(sels[k] * w_rows[k], axis=-1, keepdims=True)
            ys.append((wc * _dot(hdn.astype(BF16), wd_ref[k])).astype(BF16))
        sel_t = jnp.concatenate([jnp.where(key_cols[k] - base == c_io, 1.0, 0.0) for k in range(MOE_PAIR)],
                                axis=1).astype(BF16)
        acc_ref[...] += _dot(sel_t, jnp.concatenate(ys, axis=0))
        return carry

    n_chunks = (functools.reduce(jnp.maximum, counts) + chunk - 1) // chunk
    lax.fori_loop(0, n_chunks, one_chunk, 0)

    @pl.when(j == pl.num_programs(1) - 1)
    def _():
        o_ref[...] = _layer_norm(DEEPNORM_ALPHA * x1_ref[...] + acc_ref[...], g_ref[...], b_ref[...])


def _moe(x1, cmb, wg, wu, wd, g, b):
    n = x1.shape[0]
    tm = min(n, MOE_TILE)
    chunk = min(tm, MOE_CHUNK)
    tri = jnp.where(jnp.arange(tm)[:, None] > jnp.arange(tm)[None, :], 1.0, 0.0).astype(BF16)
    row = lambda i, j: (i, 0)
    const = lambda i, j: (0, 0)
    return pl.pallas_call(
        functools.partial(_moe_body, chunk=chunk),
        grid=(n // tm, N_EXPERTS // MOE_PAIR),
        in_specs=[pl.BlockSpec((tm, D_MODEL), row), pl.BlockSpec((tm, LANES), row),
                  pl.BlockSpec((tm, tm), const),
                  pl.BlockSpec((MOE_PAIR, D_MODEL, D_EXPERT), lambda i, j: (j, 0, 0)),
                  pl.BlockSpec((MOE_PAIR, D_MODEL, D_EXPERT), lambda i, j: (j, 0, 0)),
                  pl.BlockSpec((MOE_PAIR, D_EXPERT, D_MODEL), lambda i, j: (j, 0, 0)),
                  pl.BlockSpec((1, D_MODEL), const), pl.BlockSpec((1, D_MODEL), const)],
        out_specs=pl.BlockSpec((tm, D_MODEL), row),
        out_shape=jax.ShapeDtypeStruct((n, D_MODEL), F32),
        scratch_shapes=[pltpu.VMEM((tm, D_MODEL), F32), pltpu.VMEM((tm, D_MODEL), BF16),
                        pltpu.VMEM((tm, LANES), F32), pltpu.VMEM((LANES, tm), F32), pltpu.VMEM((LANES, tm), F32)],
        compiler_params=_cparams(("parallel", "arbitrary")),
        name="moe",
    )(x1, cmb, tri, wg, wu, wd, g, b)


def _prep_weights(w_in, gla_w_a, gla_b_a, nsa_gate_b, cmp_pe, cmp_w1, cmp_w2, gla_norm, nsa_norm, w_o,
                  ln1_g, ln1_b, router_g_w, router_g_b, router_e_w, router_e_b, w_gate, w_up, w_down,
                  ln2_g, ln2_b, n_cmp):
    o_glr, o_gog, o_nq, o_nkv, o_ngt = 1024, 1040, 1552, 2064, 2832
    pad = jnp.zeros((D_MODEL, LANES - GLA_RANK - 3 * NSA_HEADS), F32)
    w_proj = jnp.concatenate([w_in[:, 0:512], w_in[:, 512:1024], w_in[:, o_gog:o_gog + 512],
                              w_in[:, o_nq:o_nq + 512], w_in[:, o_nkv:o_nkv + 768],
                              w_in[:, o_glr:o_glr + GLA_RANK], w_in[:, o_ngt:o_ngt + 24], pad], axis=1).astype(BF16)
    wa = jnp.zeros((LANES, GLA_HEADS * GLA_DK), F32).at[0:GLA_RANK].set(gla_w_a)
    wa_hi, wa_lo = _split(wa)
    ba = gla_b_a.reshape(1, -1)
    gate_b = jnp.zeros((1, LANES), F32).at[0, SM_GATE_OFF:SM_GATE_OFF + 24].set(nsa_gate_b)
    eye = jnp.eye(NSA_GROUPS, dtype=F32)
    w1r = cmp_w1.reshape(2, CMP_LEN, NSA_HD, CMP_HIDDEN)

    def split_w1(w):
        wa_ = jnp.einsum('ldj,gh->lgdhj', w[:CMP_STRIDE], eye).reshape(CMP_STRIDE * LANES, 2 * CMP_HIDDEN)
        wb_ = jnp.einsum('ldj,gh->lgdhj', w[CMP_STRIDE:], eye).reshape(CMP_STRIDE * LANES, 2 * CMP_HIDDEN)
        return jnp.concatenate([wa_, wb_], axis=1).astype(BF16)

    wk, wv = split_w1(w1r[0]), split_w1(w1r[1])
    pe = cmp_pe.reshape(2, CMP_LEN * NSA_HD)
    w1_hi, w1_lo = _split(cmp_w1)
    w2bd = jnp.einsum('sjd,gh->sgjhd', cmp_w2, eye).reshape(2, 2 * CMP_HIDDEN, LANES).astype(BF16)
    cw = (wk, wv, pe, w1_hi, w1_lo, w2bd)
    starts = np.arange(n_cmp) * CMP_STRIDE
    sel_starts = np.arange(N_SLC_PAD) * SLC_LEN
    agg_t = ((starts[None, :] < sel_starts[:, None] + SLC_LEN)
             & (starts[None, :] + CMP_LEN > sel_starts[:, None])
             & (np.arange(n_cmp)[None, :] < n_cmp - 1))
    agg_t = jnp.asarray(agg_t, BF16)
    rw = jnp.zeros((D_MODEL, LANES), F32)
    rw = rw.at[:, 0:N_EXPERTS].set(router_e_w.transpose(1, 0, 2).reshape(D_MODEL, N_EXPERTS))
    rw = rw.at[:, N_EXPERTS:N_EXPERTS + N_EXPERT_GROUPS].set(router_g_w)
    rw_hi, rw_lo = _split(rw)
    rb = jnp.zeros((1, LANES), F32).at[0, 0:N_EXPERTS].set(router_e_b.reshape(-1))
    rb = rb.at[0, N_EXPERTS:N_EXPERTS + N_EXPERT_GROUPS].set(router_g_b)
    return dict(w_proj=w_proj, wa_hi=wa_hi, wa_lo=wa_lo, ba=ba, gate_b=gate_b, cw=cw, agg_t=agg_t,
                gn=gla_norm.reshape(1, -1), nn=nsa_norm.reshape(1, -1), wo=w_o.astype(BF16),
                ln1_g=ln1_g.reshape(1, -1), ln1_b=ln1_b.reshape(1, -1), rw_hi=rw_hi, rw_lo=rw_lo, rb=rb,
                wg=w_gate.astype(BF16), wu=w_up.astype(BF16), wd=w_down.astype(BF16),
                ln2_g=ln2_g.reshape(1, -1), ln2_b=ln2_b.reshape(1, -1))


def _tail(w, o_gla, o_nsa, x2d):
    x1, cmb = _mix(o_gla, o_nsa, x2d, w['wo'], w['ln1_g'], w['ln1_b'], w['rw_hi'], w['rw_lo'], w['rb'])
    return _moe(x1, cmb, w['wg'], w['wu'], w['wd'], w['ln2_g'], w['ln2_b'])


def kernel(x_prompt, x_sample, cache_nsa, state_win, state_gla, page_table, w_in, gla_w_a, gla_b_a, nsa_gate_b,
           cmp_pe, cmp_w1, cmp_w2, gla_norm, nsa_norm, w_o, ln1_g, ln1_b, router_g_w, router_g_b, router_e_w,
           router_e_b, w_gate, w_up, w_down, ln2_g, ln2_b):
    assert w_in.shape[0] == 1, "single layer"
    bp, tp, _ = x_prompt.shape
    bs, ts, _ = x_sample.shape
    n_pages = page_table.shape[1]
    past_len = n_pages * PAGE_SIZE
    assert tp == past_len, "prompt and past share the compressed-block count"
    w = _prep_weights(w_in[0], gla_w_a[0], gla_b_a[0], nsa_gate_b[0], cmp_pe[0], cmp_w1[0], cmp_w2[0],
                      gla_norm[0], nsa_norm[0], w_o[0], ln1_g[0], ln1_b[0], router_g_w[0], router_g_b[0],
                      router_e_w[0], router_e_b[0], w_gate[0], w_up[0], w_down[0], ln2_g[0], ln2_b[0],
                      n_cmp=tp // CMP_STRIDE)

    xp2 = x_prompt.reshape(bp * tp, D_MODEL)
    qk, v, og, nq, rows4, win, kvb, sm = _proj(xp2, w['w_proj'])
    o_gla, gla_p = _gla(qk, v, og, sm, w['wa_hi'], w['wa_lo'], w['ba'], w['gn'], None, bp, tp)
    kcv = _compress_prompt(rows4, w['cw'], bp, tp)
    blk_of_key = jnp.arange(tp, dtype=jnp.int32) // SLC_LEN
    ebt = jnp.where(blk_of_key[:, None] == jnp.arange(N_SLC_PAD, dtype=jnp.int32)[None, :], NEG_BIG, 0.0).astype(BF16)
    o_nsa = _nsa_prompt(nq, sm, kcv, kvb.reshape(bp, tp, 512), ebt, w['agg_t'], w['gate_b'], w['nn'], bp, tp)
    y_p = _tail(w, o_gla, o_nsa, xp2).reshape(bp, tp, D_MODEL)
    w_rows = min(WINDOW, tp)
    rows_p = rows4.reshape(1, bp, tp, 4, NSA_GROUPS, NSA_HD)
    win_p = win.reshape(bp, tp, 2, NSA_GROUPS, NSA_HD)[None, :, tp - w_rows:]

    xs2 = x_sample.reshape(bs * ts, D_MODEL)
    qk, v, og, nq, rows4s, win_new, _, sm = _proj(xs2, w['w_proj'])
    o_gla, gla_s = _gla(qk, v, og, sm, w['wa_hi'], w['wa_lo'], w['ba'], w['gn'], state_gla[0], bs, ts)
    cache_t = cache_nsa[0].transpose(0, 2, 3, 4, 1).reshape(-1, 4 * NSA_GROUPS * NSA_HD, PAGE_SIZE)
    w_buf = state_win.shape[2]
    win_t = state_win[0].transpose(0, 2, 3, 4, 1).reshape(bs, 2 * NSA_GROUPS * NSA_HD, w_buf)
    kcv = _compress_sample(cache_t, page_table, w['cw'], bs, n_pages)
    tile_keys = min(n_pages, 16) * PAGE_SIZE
    eb = jnp.where(blk_of_key.reshape(-1, 1, tile_keys) == jnp.arange(N_SLC_PAD, dtype=jnp.int32)[None, :, None],
                   NEG_BIG, 0.0).astype(BF16)
    o_nsa = _nsa_sample(nq, sm, kcv, rows4s, win_new, win_t, cache_t, page_table, eb, w['agg_t'], w['gate_b'],
                        w['nn'], bs, ts, n_pages)
    y_s = _tail(w, o_gla, o_nsa, xs2).reshape(bs, ts, D_MODEL)
    rows_s = rows4s.reshape(1, bs, ts, 4, NSA_GROUPS, NSA_HD)
    win_s = jnp.concatenate([state_win[0][:, ts:], win_new.reshape(bs, ts, 2, NSA_GROUPS, NSA_HD)], axis=1)[None]

    return (y_p, y_s, rows_p, rows_s, win_p, win_s, gla_p[None], gla_s[None])
```

```python
import functools

import numpy as np
import jax
import jax.numpy as jnp
from jax import lax
from jax.experimental import pallas as pl
from jax.experimental.pallas import tpu as pltpu

F32 = jnp.float32
BF16 = jnp.bfloat16

D_MODEL = 1024
GLA_HEADS = 4
GLA_DK = 64
GLA_DV = 128
GLA_RANK = 16
GLA_TAU = 16.0
GLA_CHUNK = 64
NSA_HEADS = 8
NSA_GROUPS = 2
NSA_HPG = 4
NSA_HD = 64
CMP_LEN = 32
CMP_STRIDE = 16
CMP_HIDDEN = 128
SLC_LEN = 64
SLC_TOP = 16
WINDOW = 512
PAGE_SIZE = 128
N_EXPERT_GROUPS = 4
EXPERTS_PER_GROUP = 8
N_EXPERTS = 32
D_EXPERT = 512
DEPTH = 1
DEEPNORM_ALPHA = (2.0 * DEPTH) ** 0.25
LN_EPS = 1e-5
RMS_EPS = 1e-6
NEG_BIG = -1e30
LOG2E = 1.4426950408889634
FORCE_SCORE = 1e9

LANES = 128
KV_TILE = 512
BLK_PER_TILE = KV_TILE // SLC_LEN
N_SLC_PAD = 128
VMEM_LIMIT = 56 * 1024 * 1024
MOE_TILE = 1024
MOE_CHUNK = 128
MOE_PAIR = 2

SEG_QK, SEG_V, SEG_OG, SEG_NQ, SEG_ROWS, SEG_WIN, SEG_SM = 0, 512, 1024, 1536, 2048, 2560, 2816
D_PROJ = 2944
SM_GATE_OFF = GLA_RANK


def _cparams(sem):
    return pltpu.CompilerParams(dimension_semantics=sem, vmem_limit_bytes=VMEM_LIMIT)


def _dot(a, b):
    return jnp.dot(a, b, preferred_element_type=F32)


def _dot_nt(a, b):
    return lax.dot_general(a, b, (((1,), (1,)), ((), ())), preferred_element_type=F32)


def _dot_tn(a, b):
    return lax.dot_general(a, b, (((0,), (0,)), ((), ())), preferred_element_type=F32)


def _split(x):
    hi = x.astype(BF16)
    lo = (x - hi.astype(F32)).astype(BF16)
    return hi, lo


def _dot3(a, b_hi, b_lo):
    a_hi, a_lo = _split(a)
    return _dot(a_hi, b_hi) + (_dot(a_hi, b_lo) + _dot(a_lo, b_hi))


def _sigmoid(x):
    return 1.0 / (1.0 + jnp.exp(-x))


def _silu(x):
    return x * _sigmoid(x)


def _layer_norm(y, g, b):
    mu = jnp.mean(y, axis=-1, keepdims=True)
    d = y - mu
    var = jnp.mean(d * d, axis=-1, keepdims=True)
    return d * lax.rsqrt(var + LN_EPS) * g + b


def _proj_body(x_ref, w_ref, qk_ref, v_ref, og_ref, nq_ref, rows_ref, win_ref, kvb_ref, sm_ref, rows_t_ref=None):
    x = x_ref[...].astype(BF16)

    def seg(off, width):
        return _dot(x, w_ref[:, off:off + width])

    qk_ref[...] = seg(SEG_QK, 512)
    v_ref[...] = seg(SEG_V, 512)
    og_ref[...] = seg(SEG_OG, 512)
    nq_ref[...] = seg(SEG_NQ, 512)
    rows = seg(SEG_ROWS, 512)
    rows_ref[...] = rows
    if rows_t_ref is not None:
        rows_t_ref[0] = rows.T
    win = seg(SEG_WIN, 256)
    win_ref[...] = win
    kvb_ref[:, 0:256] = rows[:, 256:512].astype(BF16)
    kvb_ref[:, 256:512] = win.astype(BF16)
    sm_ref[...] = seg(SEG_SM, 128)


def _proj(x2d, w_proj, seq_len=None):
    n = x2d.shape[0]
    tm = min(n, 512)
    widths = (512, 512, 512, 512, 512, 256, 512, 128)
    dtypes = (F32, F32, F32, F32, F32, F32, BF16, F32)
    out_specs = [pl.BlockSpec((tm, w), lambda i: (i, 0)) for w in widths]
    out_shape = [jax.ShapeDtypeStruct((n, w), d) for w, d in zip(widths, dtypes)]
    if seq_len is not None:
        nt = seq_len // tm
        out_specs.append(pl.BlockSpec((1, 512, tm), lambda i: (i // nt, 0, i % nt)))
        out_shape.append(jax.ShapeDtypeStruct((n // seq_len, 512, seq_len), F32))
    return pl.pallas_call(
        _proj_body,
        grid=(n // tm,),
        in_specs=[pl.BlockSpec((tm, D_MODEL), lambda i: (i, 0)),
                  pl.BlockSpec((D_MODEL, D_PROJ), lambda i: (0, 0))],
        out_specs=out_specs,
        out_shape=out_shape,
        compiler_params=_cparams(("parallel",)),
        name="proj",
    )(x2d, w_proj)


def _gla_body(*refs, t_real, n_chunks, has_s0):
    if has_s0:
        qk_ref, v_ref, og_ref, sm_ref, wah_ref, wal_ref, ba_ref, gn_ref, s0_ref, o_ref, sf_ref, st_ref = refs
    else:
        qk_ref, v_ref, og_ref, sm_ref, wah_ref, wal_ref, ba_ref, gn_ref, o_ref, sf_ref, st_ref = refs
        s0_ref = None
    c_len = GLA_CHUNK
    t = pl.program_id(1)

    @pl.when(t == 0)
    def _():
        for h in range(GLA_HEADS):
            if has_s0:
                st_ref[h] = s0_ref[0, h].T
            else:
                st_ref[h] = jnp.zeros((GLA_DV, GLA_DK), F32)

    r_io = lax.broadcasted_iota(jnp.int32, (c_len, c_len), 0)
    c_io = lax.broadcasted_iota(jnp.int32, (c_len, c_len), 1)
    causal = r_io >= c_io
    tri = jnp.where(causal, 1.0, 0.0).astype(BF16)
    padded = t_real < c_len

    def load(ref, rows):
        if padded:
            x = ref[...]
            return jnp.concatenate([x, jnp.zeros((c_len - t_real, x.shape[1]), x.dtype)], axis=0)
        return ref[rows, :]

    for c in range(n_chunks):
        rows = pl.ds(c * c_len, c_len)
        qk = load(qk_ref, rows)
        v = load(v_ref, rows)
        sm = load(sm_ref, rows)
        z = _dot3(sm, wah_ref[...], wal_ref[...]) + ba_ref[...]
        la = (jnp.minimum(z, 0.0) - jnp.log1p(jnp.exp(-jnp.abs(z)))) / GLA_TAU
        if padded:
            row_id = lax.broadcasted_iota(jnp.int32, la.shape, 0)
            la = jnp.where(row_id < t_real, la, 0.0)
        la_hi, la_lo = _split(la)
        cum = _dot(tri, la_hi) + _dot(tri, la_lo)
        last = cum[c_len - 1:c_len, :]
        e_q = jnp.exp(cum)
        e_k = jnp.exp(-cum)
        e_kd = jnp.exp(last - cum)
        e_l = jnp.exp(last)
        for h in range(GLA_HEADS):
            sl = slice(h * GLA_DK, (h + 1) * GLA_DK)
            qh = qk[:, sl] * (GLA_DK ** -0.5)
            kh = qk[:, GLA_HEADS * GLA_DK + h * GLA_DK:GLA_HEADS * GLA_DK + (h + 1) * GLA_DK]
            vh = v[:, h * GLA_DV:(h + 1) * GLA_DV].astype(BF16)
            q_dec = (qh * e_q[:, sl]).astype(BF16)
            k_inv = (kh * e_k[:, sl]).astype(BF16)
            k_dec = (kh * e_kd[:, sl]).astype(BF16)
            att = jnp.where(causal, _dot_nt(q_dec, k_inv), 0.0).astype(BF16)
            s_t = st_ref[h]
            o = _dot(att, vh) + _dot_nt(q_dec, s_t.astype(BF16))
            st_ref[h] = s_t * e_l[:, sl] + _dot_tn(vh, k_dec)
            gate = _silu(load(og_ref, rows)[:, h * GLA_DV:(h + 1) * GLA_DV])
            on = o * lax.rsqrt(jnp.mean(o * o, axis=-1, keepdims=True) + RMS_EPS) * gn_ref[...]
            res = on * gate
            if padded:
                o_ref[:, h * GLA_DV:(h + 1) * GLA_DV] = res[:t_real]
            else:
                o_ref[rows, h * GLA_DV:(h + 1) * GLA_DV] = res

    @pl.when(t == pl.num_programs(1) - 1)
    def _():
        for h in range(GLA_HEADS):
            sf_ref[0, h] = st_ref[h].T


def _gla(qk, v, og, sm, wa_hi, wa_lo, ba, gn, s0, nb, t_len):
    n = nb * t_len
    if t_len >= GLA_CHUNK:
        tt = min(t_len, 512)
        t_real = GLA_CHUNK
        n_chunks = tt // GLA_CHUNK
    else:
        tt = t_len
        t_real = t_len
        n_chunks = 1
    nt = t_len // tt
    row = lambda b, t: (b * nt + t, 0)
    const = lambda b, t: (0, 0)
    in_specs = [pl.BlockSpec((tt, 512), row), pl.BlockSpec((tt, 512), row), pl.BlockSpec((tt, 512), row),
                pl.BlockSpec((tt, LANES), row),
                pl.BlockSpec((LANES, 256), const), pl.BlockSpec((LANES, 256), const),
                pl.BlockSpec((1, 256), const), pl.BlockSpec((1, GLA_DV), const)]
    args = [qk, v, og, sm, wa_hi, wa_lo, ba, gn]
    if s0 is not None:
        in_specs.append(pl.BlockSpec((1, GLA_HEADS, GLA_DK, GLA_DV), lambda b, t: (b, 0, 0, 0)))
        args.append(s0)
    return pl.pallas_call(
        functools.partial(_gla_body, t_real=t_real, n_chunks=n_chunks, has_s0=s0 is not None),
        grid=(nb, nt),
        in_specs=in_specs,
        out_specs=[pl.BlockSpec((tt, 512), row),
                   pl.BlockSpec((1, GLA_HEADS, GLA_DK, GLA_DV), lambda b, t: (b, 0, 0, 0))],
        out_shape=[jax.ShapeDtypeStruct((n, 512), F32),
                   jax.ShapeDtypeStruct((nb, GLA_HEADS, GLA_DK, GLA_DV), F32)],
        scratch_shapes=[pltpu.VMEM((GLA_HEADS, GLA_DV, GLA_DK), F32)],
        compiler_params=_cparams(("parallel", "arbitrary")),
        name="gla",
    )(*args)


def _gelu_tanh(x):
    return 0.5 * x * (1.0 + jnp.tanh(0.7978845608028654 * (x + 0.044715 * x * x * x)))


def _compress_body(*refs, n_pages, rows_per_step, n_groups):
    if n_pages:
        pages = refs[1:1 + n_pages]
        wk_ref, wv_ref, pe_ref, w1h_ref, w1l_ref, w2_ref, out_ref, p_ref, rk_ref, rv_ref = refs[1 + n_pages:]
        for idx, page in enumerate(pages):
            rk_ref[idx * PAGE_SIZE:(idx + 1) * PAGE_SIZE, :] = page[0, 0:LANES, :].T
            rv_ref[idx * PAGE_SIZE:(idx + 1) * PAGE_SIZE, :] = page[0, LANES:2 * LANES, :].T
    else:
        rk_ref, rv_ref, wk_ref, wv_ref, pe_ref, w1h_ref, w1l_ref, w2_ref, out_ref, p_ref = refs
    j = pl.program_id(1)
    g_step = rows_per_step // CMP_STRIDE

    def gather(r):
        cols = [r[pl.ds(l, g_step, stride=CMP_STRIDE), :] for l in range(CMP_STRIDE)]
        return jnp.concatenate(cols, axis=1).astype(BF16)

    rows = pl.ds(pl.multiple_of(j * g_step, g_step), g_step)
    p_ref[rows, 0:512] = _dot(gather(rk_ref), wk_ref[...])
    p_ref[rows, 512:1024] = _dot(gather(rv_ref), wv_ref[...])

    @pl.when(j == pl.num_programs(1) - 1)
    def _():
        for s in range(2):
            p = p_ref[:, s * 512:(s + 1) * 512]
            pe = jnp.broadcast_to(pe_ref[s:s + 1, :], (8, CMP_LEN * NSA_HD))
            bias = _dot3(pe, w1h_ref[s], w1l_ref[s])[0:1, :]
            bias2 = jnp.concatenate([bias, bias], axis=1)
            h = p[:, 0:256] + pltpu.roll(p[:, 256:512], n_groups - 1, 0) + bias2
            out_ref[0, :, s * LANES:(s + 1) * LANES] = _dot(_gelu_tanh(h).astype(BF16), w2_ref[s]).astype(BF16)


def _compress_specs_tail():
    c2 = lambda b, j: (0, 0)
    c3 = lambda b, j: (0, 0, 0)
    return [pl.BlockSpec((16 * LANES, 512), c2), pl.BlockSpec((16 * LANES, 512), c2),
            pl.BlockSpec((2, CMP_LEN * NSA_HD), c2),
            pl.BlockSpec((2, CMP_LEN * NSA_HD, CMP_HIDDEN), c3),
            pl.BlockSpec((2, CMP_LEN * NSA_HD, CMP_HIDDEN), c3),
            pl.BlockSpec((2, 2 * CMP_HIDDEN, LANES), c3)]


def _compress_prompt(rows4, cw, nb, t_len):
    chunk = min(t_len, 2048)
    nj = t_len // chunk
    n_groups = t_len // CMP_STRIDE
    return pl.pallas_call(
        functools.partial(_compress_body, n_pages=0, rows_per_step=chunk, n_groups=n_groups),
        grid=(nb, nj),
        in_specs=[pl.BlockSpec((chunk, LANES), lambda b, j: (b * nj + j, 0)),
                  pl.BlockSpec((chunk, LANES), lambda b, j: (b * nj + j, 1))] + _compress_specs_tail(),
        out_specs=pl.BlockSpec((1, n_groups, 256), lambda b, j: (b, 0, 0)),
        out_shape=jax.ShapeDtypeStruct((nb, n_groups, 256), BF16),
        scratch_shapes=[pltpu.VMEM((n_groups, 1024), F32)],
        compiler_params=_cparams(("parallel", "arbitrary")),
        name="compress_prompt",
    )(rows4, rows4, *cw)


def _compress_sample(cache_t, page_table, cw, nb, n_pages):
    pages_per_step = min(n_pages, 16)
    nj = n_pages // pages_per_step
    n_groups = n_pages * PAGE_SIZE // CMP_STRIDE
    rows_per_step = pages_per_step * PAGE_SIZE

    def page_spec(k):
        return pl.BlockSpec((1, 2 * LANES, PAGE_SIZE), lambda b, j, pt: (pt[b, j * pages_per_step + k], 0, 0))

    tail = _compress_specs_tail()
    tail = [pl.BlockSpec(s.block_shape, (lambda f: (lambda b, j, pt: f(b, j)))(s.index_map)) for s in tail]
    grid_spec = pltpu.PrefetchScalarGridSpec(
        num_scalar_prefetch=1,
        grid=(nb, nj),
        in_specs=[page_spec(k) for k in range(pages_per_step)] + tail,
        out_specs=pl.BlockSpec((1, n_groups, 256), lambda b, j, pt: (b, 0, 0)),
        scratch_shapes=[pltpu.VMEM((n_groups, 1024), F32), pltpu.VMEM((rows_per_step, LANES), F32),
                        pltpu.VMEM((rows_per_step, LANES), F32)],
    )
    return pl.pallas_call(
        functools.partial(_compress_body, n_pages=pages_per_step, rows_per_step=rows_per_step, n_groups=n_groups),
        grid_spec=grid_spec,
        out_shape=jax.ShapeDtypeStruct((nb, n_groups, 256), BF16),
        compiler_params=_cparams(("parallel", "arbitrary")),
        name="compress_sample",
    )(page_table, *([cache_t] * pages_per_step), *cw)


def _rep_rows(x, n):
    return jnp.concatenate([x] * n, axis=0)


def _qbd(nq, qb):
    zero = jnp.zeros((NSA_HPG * qb, NSA_HD), F32)
    blocks = []
    for g in range(NSA_GROUPS):
        parts = [nq[:, (g * NSA_HPG + h) * NSA_HD:(g * NSA_HPG + h + 1) * NSA_HD] for h in range(NSA_HPG)]
        qs = jnp.concatenate(parts, axis=0) * (NSA_HD ** -0.5 * LOG2E)
        blocks.append(jnp.concatenate([qs, zero] if g == 0 else [zero, qs], axis=1))
    return jnp.concatenate(blocks, axis=0).astype(BF16)


def _compressed_branch(qbd, kc, vc, pos_c, qb):
    nc = kc.shape[0]
    cmp_end = lax.broadcasted_iota(jnp.int32, (qb, nc), 1) * CMP_STRIDE + (CMP_LEN - 1)
    valid = jnp.where(cmp_end <= pos_c, 1.0, 0.0)
    s = _dot_nt(qbd, kc) + _rep_rows(jnp.where(valid > 0.5, 0.0, NEG_BIG), NSA_HEADS)
    e = jnp.exp2(s - jnp.max(s, axis=-1, keepdims=True)) * _rep_rows(valid, NSA_HEADS)
    p = e / jnp.maximum(jnp.sum(e, axis=-1, keepdims=True), 1e-30)
    return _dot(p.astype(BF16), vc), p


def _head_sum(p, g, qb):
    r0 = g * NSA_HPG * qb
    out = p[r0:r0 + qb]
    for h in range(1, NSA_HPG):
        out = out + p[r0 + h * qb:r0 + (h + 1) * qb]
    return out


def _select_blocks(imp_t, pos_r, n_pick, n_blocks):
    shape = imp_t.shape
    s_io = lax.broadcasted_iota(jnp.int32, shape, 0)
    cur = jnp.right_shift(pos_r, 6)
    valid = (s_io <= cur) & (s_io < n_blocks)
    forced = (s_io == 0) | (s_io == cur) | (s_io == cur - 1)
    bits = pltpu.bitcast(imp_t, jnp.int32)
    packed = pltpu.bitcast((bits & jnp.int32(-N_SLC_PAD)) | (N_SLC_PAD - 1 - s_io), F32)
    key = jnp.where(valid & jnp.logical_not(forced), packed, -1.0)
    sel = jnp.where(valid & forced, 1.0, 0.0)
    for _ in range(n_pick):
        m = jnp.max(key, axis=0, keepdims=True)
        pick = (key == m) & (m >= 0.0)
        sel = jnp.where(pick, 1.0, sel)
        key = jnp.where(pick, -1.0, key)
    return sel


def _unselected(p_sum, agg_t, pos_r, n_pick):
    hi, lo = _split(p_sum)
    imp_t = _dot_nt(agg_t, hi) + _dot_nt(agg_t, lo)
    n_blocks = p_sum.shape[1] * CMP_STRIDE // SLC_LEN
    sel_t = _select_blocks(imp_t, pos_r, n_pick, n_blocks)
    return (1.0 - sel_t.T).astype(BF16)


def _online_update(s, v, m, l, acc, v_transposed=False):
    m_new = jnp.maximum(m, jnp.max(s, axis=-1, keepdims=True))
    alpha = jnp.exp2(m - m_new)
    p = jnp.exp2(s - m_new)
    l_new = alpha * l + jnp.sum(p, axis=-1, keepdims=True)
    pv = _dot_nt(p.astype(BF16), v) if v_transposed else _dot(p.astype(BF16), v)
    return m_new, l_new, alpha * acc + pv


def _softmax_rows(s):
    e = jnp.exp2(s - jnp.max(s, axis=-1, keepdims=True))
    return e / jnp.sum(e, axis=-1, keepdims=True)


def _combine_and_store(o_ref, gsig, o_c, o_s, o_w, nn, qb):
    rows = NSA_HPG * qb
    for g in range(NSA_GROUPS):
        def gcol(j):
            cols = [gsig[:, SM_GATE_OFF + (g * NSA_HPG + h) * 3 + j:SM_GATE_OFF + (g * NSA_HPG + h) * 3 + j + 1]
                    for h in range(NSA_HPG)]
            return jnp.concatenate(cols, axis=0)

        def blk(x):
            return x[g * rows:(g + 1) * rows, g * NSA_HD:(g + 1) * NSA_HD]

        o = gcol(0) * blk(o_c) + gcol(1) * blk(o_s) + gcol(2) * blk(o_w)
        o = o * lax.rsqrt(jnp.mean(o * o, axis=-1, keepdims=True) + RMS_EPS) * nn
        for h in range(NSA_HPG):
            c0 = (g * NSA_HPG + h) * NSA_HD
            o_ref[:, c0:c0 + NSA_HD] = o[h * qb:(h + 1) * qb]


def _rep_lanes(x, n):
    return jnp.concatenate([x] * n, axis=1)


def _softmax_cols(s):
    e = jnp.exp2(s - jnp.max(s, axis=0, keepdims=True))
    return e / jnp.sum(e, axis=0, keepdims=True)


def _nsa_prompt_body(nq_ref, sm_ref, kcv_ref, kvb_ref, ebt_ref, agg_ref, gb_ref, nn_ref, o_ref, *, qb):
    i = pl.program_id(1)
    s0 = i * qb
    pos_r = s0 + lax.broadcasted_iota(jnp.int32, (1, qb), 1)
    cols = NSA_HEADS * qb
    nq_t = (nq_ref[...] * (NSA_HD ** -0.5 * LOG2E)).T
    zero = jnp.zeros((NSA_HD, qb), F32)
    q_rows = []
    for g in range(NSA_GROUPS):
        blks = [nq_t[gh * NSA_HD:(gh + 1) * NSA_HD, :] if gh // NSA_HPG == g else zero for gh in range(NSA_HEADS)]
        q_rows.append(jnp.concatenate(blks, axis=1))
    qbd_t = jnp.concatenate(q_rows, axis=0).astype(BF16)

    nc = kcv_ref.shape[1]
    cmp_end = lax.broadcasted_iota(jnp.int32, (nc, qb), 0) * CMP_STRIDE + (CMP_LEN - 1)
    valid = jnp.where(cmp_end <= pos_r, 1.0, 0.0)
    s_c = _dot(kcv_ref[0, :, 0:LANES], qbd_t) + _rep_lanes(jnp.where(valid > 0.5, 0.0, NEG_BIG), NSA_HEADS)
    e_c = jnp.exp2(s_c - jnp.max(s_c, axis=0, keepdims=True)) * _rep_lanes(valid, NSA_HEADS)
    p_c = e_c / jnp.maximum(jnp.sum(e_c, axis=0, keepdims=True), 1e-30)
    o_c = _dot_tn(kcv_ref[0, :, LANES:2 * LANES], p_c.astype(BF16))

    unsel = []
    for g in range(NSA_GROUPS):
        p_sum = p_c[:, g * NSA_HPG * qb:(g * NSA_HPG + 1) * qb]
        for h in range(1, NSA_HPG):
            p_sum = p_sum + p_c[:, (g * NSA_HPG + h) * qb:(g * NSA_HPG + h + 1) * qb]
        hi, lo = _split(p_sum)
        imp_t = _dot(agg_ref[...], hi) + _dot(agg_ref[...], lo)
        sel_t = _select_blocks(imp_t, pos_r, SLC_TOP - 3, nc * CMP_STRIDE // SLC_LEN)
        unsel += [(1.0 - sel_t).astype(BF16)] * NSA_HPG
    lhs_t = jnp.concatenate([qbd_t, jnp.concatenate(unsel, axis=1)], axis=0)

    def tile(kt, carry, causal):
        krows = pl.ds(pl.multiple_of(kt * KV_TILE, KV_TILE), KV_TILE)
        rhs = jnp.concatenate([kvb_ref[0, krows, 0:LANES], ebt_ref[krows, :]], axis=1)
        s = _dot(rhs, lhs_t)
        if causal:
            key_pos = kt * KV_TILE + lax.broadcasted_iota(jnp.int32, (KV_TILE, qb), 0)
            s = s + _rep_lanes(jnp.where(key_pos <= pos_r, 0.0, NEG_BIG), NSA_HEADS)
        m, l, acc = carry
        m_new = jnp.maximum(m, jnp.max(s, axis=0, keepdims=True))
        alpha = jnp.exp2(m - m_new)
        p = jnp.exp2(s - m_new)
        l_new = alpha * l + jnp.sum(p, axis=0, keepdims=True)
        acc_new = alpha * acc + _dot_tn(kvb_ref[0, krows, LANES:2 * LANES], p.astype(BF16))
        return m_new, l_new, acc_new

    n_tiles = (s0 + qb - 1) // KV_TILE + 1
    init = (jnp.full((1, cols), NEG_BIG, F32), jnp.zeros((1, cols), F32), jnp.zeros((LANES, cols), F32))
    carry = lax.fori_loop(0, n_tiles - 1, lambda kt, c: tile(kt, c, False), init)
    _, l, acc = tile(n_tiles - 1, carry, True)
    o_s = acc / l

    w_len = WINDOW + qb
    w_start = pl.multiple_of(jnp.maximum(s0 - WINDOW, 0), qb)
    wrows = pl.ds(w_start, w_len)
    pos_w = w_start + lax.broadcasted_iota(jnp.int32, (w_len, qb), 0)
    ok = (pos_w <= pos_r) & (pos_r - pos_w < WINDOW)
    s_w = _dot(kvb_ref[0, wrows, 2 * LANES:3 * LANES], qbd_t) + _rep_lanes(jnp.where(ok, 0.0, NEG_BIG), NSA_HEADS)
    o_w = _dot_tn(kvb_ref[0, wrows, 3 * LANES:4 * LANES], _softmax_cols(s_w).astype(BF16))

    gsig_t = _sigmoid(sm_ref[...] + gb_ref[...]).T
    for pair in range(NSA_HEADS // 2):
        blocks = []
        for hh in (2 * pair, 2 * pair + 1):
            g = hh // NSA_HPG

            def blk(x):
                return x[g * NSA_HD:(g + 1) * NSA_HD, hh * qb:(hh + 1) * qb]

            def gate(j):
                r = SM_GATE_OFF + hh * 3 + j
                return gsig_t[r:r + 1, :]

            o = gate(0) * blk(o_c) + gate(1) * blk(o_s) + gate(2) * blk(o_w)
            blocks.append(o * lax.rsqrt(jnp.mean(o * o, axis=0, keepdims=True) + RMS_EPS) * nn_ref[...])
        o_ref[:, pair * LANES:(pair + 1) * LANES] = jnp.concatenate(blocks, axis=0).T


def _nsa_prompt(nq, sm, kcv, kvb3, ebt, agg_t, gate_b, nsa_norm, nb, t_len):
    qb = 128
    nqb = t_len // qb
    nc = t_len // CMP_STRIDE
    row = lambda b, i: (b * nqb + i, 0)
    return pl.pallas_call(
        functools.partial(_nsa_prompt_body, qb=qb),
        grid=(nb, nqb),
        in_specs=[pl.BlockSpec((qb, 512), row), pl.BlockSpec((qb, LANES), row),
                  pl.BlockSpec((1, nc, 256), lambda b, i: (b, 0, 0)),
                  pl.BlockSpec((1, t_len, 512), lambda b, i: (b, 0, 0)),
                  pl.BlockSpec((t_len, N_SLC_PAD), lambda b, i: (0, 0)),
                  pl.BlockSpec((N_SLC_PAD, nc), lambda b, i: (0, 0)),
                  pl.BlockSpec((1, LANES), lambda b, i: (0, 0)),
                  pl.BlockSpec((NSA_HD, 1), lambda b, i: (0, 0))],
        out_specs=pl.BlockSpec((qb, 512), row),
        out_shape=jax.ShapeDtypeStruct((nb * t_len, 512), F32),
        compiler_params=_cparams(("parallel", "arbitrary")),
        name="nsa_prompt",
    )(nq, sm, kcv, kvb3, ebt, agg_t, gate_b, nsa_norm)


def _nsa_sample_body(*refs, n_new, past_len, pages_per_tile, w_buf):
    pages = refs[1:1 + pages_per_tile]
    (nq_ref, sm_ref, kcv_ref, new_ref, wnew_ref, win_ref, eb_ref, agg_ref, gb_ref, nn_ref, o_ref,
     lhs_ref, oc_ref, m_ref, l_ref, acc_ref) = refs[1 + pages_per_tile:]
    kt = pl.program_id(1)
    qb = n_new
    rows = NSA_HEADS * qb
    pos_c = past_len + lax.broadcasted_iota(jnp.int32, (qb, 1), 0)

    @pl.when(kt == 0)
    def _():
        qbd = _qbd(nq_ref[...], qb)
        o_c, p_c = _compressed_branch(qbd, kcv_ref[0, :, 0:LANES], kcv_ref[0, :, LANES:2 * LANES], pos_c, qb)
        oc_ref[...] = o_c
        pos_r = past_len + lax.broadcasted_iota(jnp.int32, (1, N_SLC_PAD), 1)
        unsel = []
        for g in range(NSA_GROUPS):
            p_sum = _head_sum(p_c, g, qb)
            p_pad = jnp.concatenate([p_sum, jnp.zeros((N_SLC_PAD - qb, p_sum.shape[1]), F32)], axis=0)
            unsel += [_unselected(p_pad, agg_ref[...], pos_r, SLC_TOP - 3)[0:qb]] * NSA_HPG
        lhs_ref[...] = jnp.concatenate([qbd, jnp.concatenate(unsel, axis=0)], axis=1)
        m_ref[...] = jnp.full((rows, 1), NEG_BIG, F32)
        l_ref[...] = jnp.zeros((rows, 1), F32)
        acc_ref[...] = jnp.zeros((rows, LANES), F32)

    k_t = jnp.concatenate([p[0, 0:LANES, :] for p in pages], axis=1).astype(BF16)
    v_t = jnp.concatenate([p[0, LANES:2 * LANES, :] for p in pages], axis=1).astype(BF16)
    s = _dot(lhs_ref[...], jnp.concatenate([k_t, eb_ref[0]], axis=0))
    m, l, acc = _online_update(s, v_t, m_ref[...], l_ref[...], acc_ref[...], v_transposed=True)
    m_ref[...] = m
    l_ref[...] = l
    acc_ref[...] = acc

    @pl.when(kt == pl.num_programs(1) - 1)
    def _():
        qbd = lhs_ref[:, 0:LANES]
        pad = jnp.zeros((LANES - n_new, LANES), F32)
        q_idx = lax.broadcasted_iota(jnp.int32, (qb, LANES), 0)
        lane = lax.broadcasted_iota(jnp.int32, (qb, LANES), 1)
        new_ok = _rep_rows(jnp.where((lane < n_new) & (lane <= q_idx), 0.0, NEG_BIG), NSA_HEADS)
        new = new_ref[...]
        k_new = jnp.concatenate([new[:, 2 * LANES:3 * LANES], pad], axis=0).astype(BF16)
        v_new = jnp.concatenate([new[:, 3 * LANES:4 * LANES], pad], axis=0).astype(BF16)
        _, l_f, acc_f = _online_update(_dot_nt(qbd, k_new) + new_ok, v_new, m, l, acc)
        o_s = acc_f / l_f
        pos_w = (past_len - w_buf) + lax.broadcasted_iota(jnp.int32, (1, w_buf), 1)
        ok = (pos_w <= pos_c) & (pos_c - pos_w < WINDOW) & (pos_w >= 0)
        s_w = _dot(qbd, win_ref[0, 0:LANES, :].astype(BF16)) + _rep_rows(jnp.where(ok, 0.0, NEG_BIG), NSA_HEADS)
        wnew = wnew_ref[...]
        kw_new = jnp.concatenate([wnew[:, 0:LANES], pad], axis=0).astype(BF16)
        vw_new = jnp.concatenate([wnew[:, LANES:2 * LANES], pad], axis=0).astype(BF16)
        s_w2 = _dot_nt(qbd, kw_new) + new_ok
        p_w = _softmax_rows(jnp.concatenate([s_w, s_w2], axis=1)).astype(BF16)
        o_w = _dot_nt(p_w[:, 0:w_buf], win_ref[0, LANES:2 * LANES, :].astype(BF16)) + _dot(p_w[:, w_buf:], vw_new)
        gsig = _sigmoid(sm_ref[...] + gb_ref[...])
        _combine_and_store(o_ref, gsig, oc_ref[...], o_s, o_w, nn_ref[...], qb)


def _nsa_sample(nq, sm, kcv, rows_new, win_new, win_t, cache_t, page_table, eb, agg_t, gate_b, nsa_norm,
                nb, n_new, n_pages):
    past_len = n_pages * PAGE_SIZE
    pages_per_tile = min(n_pages, 16)
    n_tiles = n_pages // pages_per_tile
    nc = past_len // CMP_STRIDE
    w_buf = win_t.shape[2]
    rows = NSA_HEADS * n_new

    def page_spec(k):
        return pl.BlockSpec((1, 2 * LANES, PAGE_SIZE), lambda b, t, pt: (pt[b, t * pages_per_tile + k], 1, 0))

    row = lambda b, t, pt: (b, 0)
    const2 = lambda b, t, pt: (0, 0)
    grid_spec = pltpu.PrefetchScalarGridSpec(
        num_scalar_prefetch=1,
        grid=(nb, n_tiles),
        in_specs=[page_spec(k) for k in range(pages_per_tile)] + [
            pl.BlockSpec((n_new, 512), row), pl.BlockSpec((n_new, LANES), row),
            pl.BlockSpec((1, nc, 256), lambda b, t, pt: (b, 0, 0)),
            pl.BlockSpec((n_new, 512), row), pl.BlockSpec((n_new, 256), row),
            pl.BlockSpec((1, 256, w_buf), lambda b, t, pt: (b, 0, 0)),
            pl.BlockSpec((1, N_SLC_PAD, pages_per_tile * PAGE_SIZE), lambda b, t, pt: (t, 0, 0)),
            pl.BlockSpec((N_SLC_PAD, nc), const2),
            pl.BlockSpec((1, LANES), const2), pl.BlockSpec((1, NSA_HD), const2)],
        out_specs=pl.BlockSpec((n_new, 512), row),
        scratch_shapes=[pltpu.VMEM((rows, 2 * LANES), BF16), pltpu.VMEM((rows, LANES), F32),
                        pltpu.VMEM((rows, 1), F32), pltpu.VMEM((rows, 1), F32), pltpu.VMEM((rows, LANES), F32)],
    )
    return pl.pallas_call(
        functools.partial(_nsa_sample_body, n_new=n_new, past_len=past_len, pages_per_tile=pages_per_tile,
                          w_buf=w_buf),
        grid_spec=grid_spec,
        out_shape=jax.ShapeDtypeStruct((nb * n_new, 512), F32),
        compiler_params=_cparams(("parallel", "arbitrary")),
        name="nsa_sample",
    )(page_table, *([cache_t] * pages_per_tile), nq, sm, kcv, rows_new, win_new, win_t, eb, agg_t, gate_b, nsa_norm)


def _route(logits):
    lane = lax.broadcasted_iota(jnp.int32, logits.shape, 1)
    lane_f = lane.astype(F32)
    big = 1e6
    is_g = (lane >= N_EXPERTS) & (lane < N_EXPERTS + N_EXPERT_GROUPS)
    gl = jnp.where(is_g, logits, NEG_BIG)
    gmax = jnp.max(gl, axis=-1, keepdims=True)
    gtop = jnp.min(jnp.where(gl == gmax, lane_f, big), axis=-1, keepdims=True) - N_EXPERTS
    p_group = 1.0 / jnp.sum(jnp.exp(gl - gmax), axis=-1, keepdims=True)
    in_grp = (lane < N_EXPERTS) & (jnp.right_shift(lane, 3).astype(F32) == gtop)
    el = jnp.where(in_grp, logits, NEG_BIG)
    m1 = jnp.max(el, axis=-1, keepdims=True)
    i1 = jnp.min(jnp.where(el == m1, lane_f, big), axis=-1, keepdims=True)
    el2 = jnp.where(lane_f == i1, NEG_BIG, el)
    m2 = jnp.max(el2, axis=-1, keepdims=True)
    i2 = jnp.min(jnp.where(el2 == m2, lane_f, big), axis=-1, keepdims=True)
    r = jnp.exp(m2 - m1)
    t1 = 1.0 / (1.0 + r)
    t2 = r * t1
    return jnp.where(lane_f == i1, p_group * t1, 0.0) + jnp.where(lane_f == i2, p_group * t2, 0.0)


def _mix_body(og_ref, on_ref, x_ref, wo_ref, g_ref, b_ref, rwh_ref, rwl_ref, rb_ref, x1_ref, cmb_ref):
    mix = _dot(og_ref[...].astype(BF16), wo_ref[0:512, :]) + _dot(on_ref[...].astype(BF16), wo_ref[512:1024, :])
    x1 = _layer_norm(DEEPNORM_ALPHA * x_ref[...] + mix, g_ref[...], b_ref[...])
    x1_ref[...] = x1
    logits = _dot3(x1, rwh_ref[...], rwl_ref[...]) + rb_ref[...]
    cmb_ref[...] = _route(logits)


def _mix(o_gla, o_nsa, x2d, wo, g, b, rw_hi, rw_lo, rb):
    n = x2d.shape[0]
    tm = min(n, 512)
    row = lambda i: (i, 0)
    const = lambda i: (0, 0)
    return pl.pallas_call(
        _mix_body,
        grid=(n // tm,),
        in_specs=[pl.BlockSpec((tm, 512), row), pl.BlockSpec((tm, 512), row), pl.BlockSpec((tm, D_MODEL), row),
                  pl.BlockSpec((D_MODEL, D_MODEL), const), pl.BlockSpec((1, D_MODEL), const),
                  pl.BlockSpec((1, D_MODEL), const), pl.BlockSpec((D_MODEL, LANES), const),
                  pl.BlockSpec((D_MODEL, LANES), const), pl.BlockSpec((1, LANES), const)],
        out_specs=[pl.BlockSpec((tm, D_MODEL), row), pl.BlockSpec((tm, LANES), row)],
        out_shape=[jax.ShapeDtypeStruct((n, D_MODEL), F32), jax.ShapeDtypeStruct((n, LANES), F32)],
        compiler_params=_cparams(("parallel",)),
        name="mix",
    )(o_gla, o_nsa, x2d, wo, g, b, rw_hi, rw_lo, rb)


def _moe_body(x1_ref, cmb_ref, tri_ref, wg_ref, wu_ref, wd_ref, g_ref, b_ref, o_ref,
              acc_ref, xb_ref, key_ref, keyt_ref, cmbt_ref, *, chunk):
    j = pl.program_id(1)
    tm = x1_ref.shape[0]

    @pl.when(j == 0)
    def _():
        acc_ref[...] = jnp.zeros_like(acc_ref)
        xb_ref[...] = x1_ref[...].astype(BF16)
        cmb = cmb_ref[...]
        routed = cmb != 0.0
        rank = _dot(tri_ref[...], jnp.where(routed, 1.0, 0.0).astype(BF16))
        key = jnp.where(routed, rank, -1.0)
        key_ref[...] = key
        keyt_ref[...] = key.T
        cmbt_ref[...] = cmb.T

    lane = lax.broadcasted_iota(jnp.int32, (tm, LANES), 1)
    r_io = lax.broadcasted_iota(jnp.int32, (chunk, tm), 0).astype(F32)
    c_io = lax.broadcasted_iota(jnp.int32, (tm, chunk), 1).astype(F32)
    key_rows, w_rows, key_cols, counts = [], [], [], []
    for k in range(MOE_PAIR):
        e = MOE_PAIR * j + k
        key_rows.append(keyt_ref[pl.ds(e, 1), :])
        w_rows.append(cmbt_ref[pl.ds(e, 1), :])
        key_cols.append(jnp.sum(jnp.where(lane == e, key_ref[...], 0.0), axis=-1, keepdims=True))
        counts.append(jnp.sum(jnp.where(key_rows[k] >= 0.0, 1.0, 0.0)).astype(jnp.int32))

    def one_chunk(c, carry):
        base = (c * chunk).astype(F32)
        sels = [jnp.where(key_rows[k] - base == r_io, 1.0, 0.0) for k in range(MOE_PAIR)]
        xe = _dot(jnp.concatenate(sels, axis=0).astype(BF16), xb_ref[...]).astype(BF16)
        ys = []
        for k in range(MOE_PAIR):
            xk = xe[k * chunk:(k + 1) * chunk]
            hdn = _silu(_dot(xk, wg_ref[k])) * _dot(xk, wu_ref[k])
            wc = jnp.sum(sels[k] * w_rows[k], axis=-1, keepdims=True)
            ys.append((wc * _dot(hdn.astype(BF16), wd_ref[k])).astype(BF16))
        sel_t = jnp.concatenate([jnp.where(key_cols[k] - base == c_io, 1.0, 0.0) for k in range(MOE_PAIR)],
                                axis=1).astype(BF16)
        acc_ref[...] += _dot(sel_t, jnp.concatenate(ys, axis=0))
        return carry

    n_chunks = (functools.reduce(jnp.maximum, counts) + chunk - 1) // chunk
    lax.fori_loop(0, n_chunks, one_chunk, 0)

    @pl.when(j == pl.num_programs(1) - 1)
    def _():
        o_ref[...] = _layer_norm(DEEPNORM_ALPHA * x1_ref[...] + acc_ref[...], g_ref[...], b_ref[...])


def _moe(x1, cmb, wg, wu, wd, g, b):
    n = x1.shape[0]
    tm = min(n, MOE_TILE)
    chunk = min(tm, MOE_CHUNK)
    tri = jnp.where(jnp.arange(tm)[:, None] > jnp.arange(tm)[None, :], 1.0, 0.0).astype(BF16)
    row = lambda i, j: (i, 0)
    const = lambda i, j: (0, 0)
    return pl.pallas_call(
        functools.partial(_moe_body, chunk=chunk),
        grid=(n // tm, N_EXPERTS // MOE_PAIR),
        in_specs=[pl.BlockSpec((tm, D_MODEL), row), pl.BlockSpec((tm, LANES), row),
                  pl.BlockSpec((tm, tm), const),
                  pl.BlockSpec((MOE_PAIR, D_MODEL, D_EXPERT), lambda i, j: (j, 0, 0)),
                  pl.BlockSpec((MOE_PAIR, D_MODEL, D_EXPERT), lambda i, j: (j, 0, 0)),
                  pl.BlockSpec((MOE_PAIR, D_EXPERT, D_MODEL), lambda i, j: (j, 0, 0)),
                  pl.BlockSpec((1, D_MODEL), const), pl.BlockSpec((1, D_MODEL), const)],
        out_specs=pl.BlockSpec((tm, D_MODEL), row),
        out_shape=jax.ShapeDtypeStruct((n, D_MODEL), F32),
        scratch_shapes=[pltpu.VMEM((tm, D_MODEL), F32), pltpu.VMEM((tm, D_MODEL), BF16),
                        pltpu.VMEM((tm, LANES), F32), pltpu.VMEM((LANES, tm), F32), pltpu.VMEM((LANES, tm), F32)],
        compiler_params=_cparams(("parallel", "arbitrary")),
        name="moe",
    )(x1, cmb, tri, wg, wu, wd, g, b)


def _prep_weights(w_in, gla_w_a, gla_b_a, nsa_gate_b, cmp_pe, cmp_w1, cmp_w2, gla_norm, nsa_norm, w_o,
                  ln1_g, ln1_b, router_g_w, router_g_b, router_e_w, router_e_b, w_gate, w_up, w_down,
                  ln2_g, ln2_b, n_cmp):
    o_glr, o_gog, o_nq, o_nkv, o_ngt = 1024, 1040, 1552, 2064, 2832
    pad = jnp.zeros((D_MODEL, LANES - GLA_RANK - 3 * NSA_HEADS), F32)
    w_proj = jnp.concatenate([w_in[:, 0:512], w_in[:, 512:1024], w_in[:, o_gog:o_gog + 512],
                              w_in[:, o_nq:o_nq + 512], w_in[:, o_nkv:o_nkv + 768],
                              w_in[:, o_glr:o_glr + GLA_RANK], w_in[:, o_ngt:o_ngt + 24], pad], axis=1).astype(BF16)
    wa = jnp.zeros((LANES, GLA_HEADS * GLA_DK), F32).at[0:GLA_RANK].set(gla_w_a)
    wa_hi, wa_lo = _split(wa)
    ba = gla_b_a.reshape(1, -1)
    gate_b = jnp.zeros((1, LANES), F32).at[0, SM_GATE_OFF:SM_GATE_OFF + 24].set(nsa_gate_b)
    eye = jnp.eye(NSA_GROUPS, dtype=F32)
    w1r = cmp_w1.reshape(2, CMP_LEN, NSA_HD, CMP_HIDDEN)

    def split_w1(w):
        wa_ = jnp.einsum('ldj,gh->lgdhj', w[:CMP_STRIDE], eye).reshape(CMP_STRIDE * LANES, 2 * CMP_HIDDEN)
        wb_ = jnp.einsum('ldj,gh->lgdhj', w[CMP_STRIDE:], eye).reshape(CMP_STRIDE * LANES, 2 * CMP_HIDDEN)
        return jnp.concatenate([wa_, wb_], axis=1).astype(BF16)

    wk, wv = split_w1(w1r[0]), split_w1(w1r[1])
    pe = cmp_pe.reshape(2, CMP_LEN * NSA_HD)
    w1_hi, w1_lo = _split(cmp_w1)
    w2bd = jnp.einsum('sjd,gh->sgjhd', cmp_w2, eye).reshape(2, 2 * CMP_HIDDEN, LANES).astype(BF16)
    cw = (wk, wv, pe, w1_hi, w1_lo, w2bd)
    starts = np.arange(n_cmp) * CMP_STRIDE
    sel_starts = np.arange(N_SLC_PAD) * SLC_LEN
    agg_t = ((starts[None, :] < sel_starts[:, None] + SLC_LEN)
             & (starts[None, :] + CMP_LEN > sel_starts[:, None])
             & (np.arange(n_cmp)[None, :] < n_cmp - 1))
    agg_t = jnp.asarray(agg_t, BF16)
    rw = jnp.zeros((D_MODEL, LANES), F32)
    rw = rw.at[:, 0:N_EXPERTS].set(router_e_w.transpose(1, 0, 2).reshape(D_MODEL, N_EXPERTS))
    rw = rw.at[:, N_EXPERTS:N_EXPERTS + N_EXPERT_GROUPS].set(router_g_w)
    rw_hi, rw_lo = _split(rw)
    rb = jnp.zeros((1, LANES), F32).at[0, 0:N_EXPERTS].set(router_e_b.reshape(-1))
    rb = rb.at[0, N_EXPERTS:N_EXPERTS + N_EXPERT_GROUPS].set(router_g_b)
    return dict(w_proj=w_proj, wa_hi=wa_hi, wa_lo=wa_lo, ba=ba, gate_b=gate_b, cw=cw, agg_t=agg_t,
                gn=gla_norm.reshape(1, -1), nn=nsa_norm.reshape(1, -1), wo=w_o.astype(BF16),
                ln1_g=ln1_g.reshape(1, -1), ln1_b=ln1_b.reshape(1, -1), rw_hi=rw_hi, rw_lo=rw_lo, rb=rb,
                wg=w_gate.astype(BF16), wu=w_up.astype(BF16), wd=w_down.astype(BF16),
                ln2_g=ln2_g.reshape(1, -1), ln2_b=ln2_b.reshape(1, -1))


def _tail(w, o_gla, o_nsa, x2d):
    x1, cmb = _mix(o_gla, o_nsa, x2d, w['wo'], w['ln1_g'], w['ln1_b'], w['rw_hi'], w['rw_lo'], w['rb'])
    return _moe(x1, cmb, w['wg'], w['wu'], w['wd'], w['ln2_g'], w['ln2_b'])


def kernel(x_prompt, x_sample, cache_nsa, state_win, state_gla, page_table, w_in, gla_w_a, gla_b_a, nsa_gate_b,
           cmp_pe, cmp_w1, cmp_w2, gla_norm, nsa_norm, w_o, ln1_g, ln1_b, router_g_w, router_g_b, router_e_w,
           router_e_b, w_gate, w_up, w_down, ln2_g, ln2_b):
    assert w_in.shape[0] == 1, "single layer"
    bp, tp, _ = x_prompt.shape
    bs, ts, _ = x_sample.shape
    n_pages = page_table.shape[1]
    past_len = n_pages * PAGE_SIZE
    assert tp == past_len, "prompt and past share the compressed-block count"
    w = _prep_weights(w_in[0], gla_w_a[0], gla_b_a[0], nsa_gate_b[0], cmp_pe[0], cmp_w1[0], cmp_w2[0],
                      gla_norm[0], nsa_norm[0], w_o[0], ln1_g[0], ln1_b[0], router_g_w[0], router_g_b[0],
                      router_e_w[0], router_e_b[0], w_gate[0], w_up[0], w_down[0], ln2_g[0], ln2_b[0],
                      n_cmp=tp // CMP_STRIDE)

    xp2 = x_prompt.reshape(bp * tp, D_MODEL)
    qk, v, og, nq, rows4, win, kvb, sm, rows_t = _proj(xp2, w['w_proj'], seq_len=tp)
    o_gla, gla_p = _gla(qk, v, og, sm, w['wa_hi'], w['wa_lo'], w['ba'], w['gn'], None, bp, tp)
    kcv = _compress_prompt(rows4, w['cw'], bp, tp)
    blk_of_key = jnp.arange(tp, dtype=jnp.int32) // SLC_LEN
    ebt = jnp.where(blk_of_key[:, None] == jnp.arange(N_SLC_PAD, dtype=jnp.int32)[None, :], NEG_BIG, 0.0).astype(BF16)
    o_nsa = _nsa_prompt(nq, sm, kcv, kvb.reshape(bp, tp, 512), ebt, w['agg_t'], w['gate_b'], w['nn'].reshape(-1, 1),
                        bp, tp)
    y_p = _tail(w, o_gla, o_nsa, xp2).reshape(bp, tp, D_MODEL)
    w_rows = min(WINDOW, tp)
    rows_p = rows_t.reshape(bp, 4, NSA_GROUPS, NSA_HD, tp).transpose(0, 4, 1, 2, 3)[None]
    win_p = win.reshape(bp, tp, 2, NSA_GROUPS, NSA_HD)[None, :, tp - w_rows:]

    xs2 = x_sample.reshape(bs * ts, D_MODEL)
    qk, v, og, nq, rows4s, win_new, _, sm = _proj(xs2, w['w_proj'])
    o_gla, gla_s = _gla(qk, v, og, sm, w['wa_hi'], w['wa_lo'], w['ba'], w['gn'], state_gla[0], bs, ts)
    cache_t = cache_nsa[0].transpose(0, 2, 3, 4, 1).reshape(-1, 4 * NSA_GROUPS * NSA_HD, PAGE_SIZE)
    w_buf = state_win.shape[2]
    win_t = state_win[0].transpose(0, 2, 3, 4, 1).reshape(bs, 2 * NSA_GROUPS * NSA_HD, w_buf)
    kcv = _compress_sample(cache_t, page_table, w['cw'], bs, n_pages)
    tile_keys = min(n_pages, 16) * PAGE_SIZE
    eb = jnp.where(blk_of_key.reshape(-1, 1, tile_keys) == jnp.arange(N_SLC_PAD, dtype=jnp.int32)[None, :, None],
                   NEG_BIG, 0.0).astype(BF16)
    o_nsa = _nsa_sample(nq, sm, kcv, rows4s, win_new, win_t, cache_t, page_table, eb, w['agg_t'], w['gate_b'],
                        w['nn'], bs, ts, n_pages)
    y_s = _tail(w, o_gla, o_nsa, xs2).reshape(bs, ts, D_MODEL)
    rows_s = rows4s.reshape(1, bs, ts, 4, NSA_GROUPS, NSA_HD)
    win_s = jnp.concatenate([state_win[0][:, ts:], win_new.reshape(bs, ts, 2, NSA_GROUPS, NSA_HD)], axis=1)[None]

    return (y_p, y_s, rows_p, rows_s, win_p, win_s, gla_p[None], gla_s[None])
```

```python
import functools

import numpy as np
import jax
import jax.numpy as jnp
from jax import lax
from jax.experimental import pallas as pl
from jax.experimental.pallas import tpu as pltpu

F32 = jnp.float32
BF16 = jnp.bfloat16

D_MODEL = 1024
GLA_HEADS = 4
GLA_DK = 64
GLA_DV = 128
GLA_RANK = 16
GLA_TAU = 16.0
GLA_CHUNK = 64
NSA_HEADS = 8
NSA_GROUPS = 2
NSA_HPG = 4
NSA_HD = 64
CMP_LEN = 32
CMP_STRIDE = 16
CMP_HIDDEN = 128
SLC_LEN = 64
SLC_TOP = 16
WINDOW = 512
PAGE_SIZE = 128
N_EXPERT_GROUPS = 4
EXPERTS_PER_GROUP = 8
N_EXPERTS = 32
D_EXPERT = 512
DEPTH = 1
DEEPNORM_ALPHA = (2.0 * DEPTH) ** 0.25
LN_EPS = 1e-5
RMS_EPS = 1e-6
NEG_BIG = -1e30
LOG2E = 1.4426950408889634
FORCE_SCORE = 1e9

LANES = 128
KV_TILE = 512
BLK_PER_TILE = KV_TILE // SLC_LEN
N_SLC_PAD = 128
VMEM_LIMIT = 56 * 1024 * 1024
MOE_TILE = 1024
MOE_CHUNK = 128
MOE_PAIR = 2

SEG_QK, SEG_V, SEG_OG, SEG_NQ, SEG_ROWS, SEG_WIN, SEG_SM = 0, 512, 1024, 1536, 2048, 2560, 2816
D_PROJ = 2944
SM_GATE_OFF = GLA_RANK


def _cparams(sem):
    return pltpu.CompilerParams(dimension_semantics=sem, vmem_limit_bytes=VMEM_LIMIT)


def _dot(a, b):
    return jnp.dot(a, b, preferred_element_type=F32)


def _dot_nt(a, b):
    return lax.dot_general(a, b, (((1,), (1,)), ((), ())), preferred_element_type=F32)


def _dot_tn(a, b):
    return lax.dot_general(a, b, (((0,), (0,)), ((), ())), preferred_element_type=F32)


def _split(x):
    hi = x.astype(BF16)
    lo = (x - hi.astype(F32)).astype(BF16)
    return hi, lo


def _dot3(a, b_hi, b_lo):
    a_hi, a_lo = _split(a)
    return _dot(a_hi, b_hi) + (_dot(a_hi, b_lo) + _dot(a_lo, b_hi))


def _sigmoid(x):
    return 1.0 / (1.0 + jnp.exp(-x))


def _silu(x):
    return x * _sigmoid(x)


def _layer_norm(y, g, b):
    mu = jnp.mean(y, axis=-1, keepdims=True)
    d = y - mu
    var = jnp.mean(d * d, axis=-1, keepdims=True)
    return d * lax.rsqrt(var + LN_EPS) * g + b


def _proj_body(x_ref, w_ref, qk_ref, v_ref, og_ref, nq_ref, rows_ref, win_ref, kvb_ref, sm_ref, rows_t_ref=None):
    x = x_ref[...].astype(BF16)

    def seg(off, width):
        return _dot(x, w_ref[:, off:off + width])

    qk_ref[...] = seg(SEG_QK, 512)
    v_ref[...] = seg(SEG_V, 512)
    og_ref[...] = seg(SEG_OG, 512)
    nq_ref[...] = seg(SEG_NQ, 512)
    rows = seg(SEG_ROWS, 512)
    rows_ref[...] = rows
    if rows_t_ref is not None:
        rows_t_ref[0] = rows.T
    win = seg(SEG_WIN, 256)
    win_ref[...] = win
    kvb_ref[:, 0:256] = rows[:, 256:512].astype(BF16)
    kvb_ref[:, 256:512] = win.astype(BF16)
    sm_ref[...] = seg(SEG_SM, 128)


def _proj(x2d, w_proj, seq_len=None):
    n = x2d.shape[0]
    tm = min(n, 512)
    widths = (512, 512, 512, 512, 512, 256, 512, 128)
    dtypes = (F32, F32, F32, F32, F32, F32, BF16, F32)
    out_specs = [pl.BlockSpec((tm, w), lambda i: (i, 0)) for w in widths]
    out_shape = [jax.ShapeDtypeStruct((n, w), d) for w, d in zip(widths, dtypes)]
    if seq_len is not None:
        nt = seq_len // tm
        out_specs.append(pl.BlockSpec((1, 512, tm), lambda i: (i // nt, 0, i % nt)))
        out_shape.append(jax.ShapeDtypeStruct((n // seq_len, 512, seq_len), F32))
    return pl.pallas_call(
        _proj_body,
        grid=(n // tm,),
        in_specs=[pl.BlockSpec((tm, D_MODEL), lambda i: (i, 0)),
                  pl.BlockSpec((D_MODEL, D_PROJ), lambda i: (0, 0))],
        out_specs=out_specs,
        out_shape=out_shape,
        compiler_params=_cparams(("parallel",)),
        name="proj",
    )(x2d, w_proj)


def _gla_body(*refs, t_real, n_chunks, has_s0):
    if has_s0:
        qk_ref, v_ref, og_ref, sm_ref, wah_ref, wal_ref, ba_ref, gn_ref, s0_ref, o_ref, sf_ref, st_ref = refs
    else:
        qk_ref, v_ref, og_ref, sm_ref, wah_ref, wal_ref, ba_ref, gn_ref, o_ref, sf_ref, st_ref = refs
        s0_ref = None
    c_len = GLA_CHUNK
    t = pl.program_id(1)

    @pl.when(t == 0)
    def _():
        for h in range(GLA_HEADS):
            if has_s0:
                st_ref[h] = s0_ref[0, h].T
            else:
                st_ref[h] = jnp.zeros((GLA_DV, GLA_DK), F32)

    r_io = lax.broadcasted_iota(jnp.int32, (c_len, c_len), 0)
    c_io = lax.broadcasted_iota(jnp.int32, (c_len, c_len), 1)
    causal = r_io >= c_io
    padded = t_real < c_len
    t_rows = n_chunks * c_len

    def load(ref, rows):
        if padded:
            x = ref[...]
            return jnp.concatenate([x, jnp.zeros((c_len - t_real, x.shape[1]), x.dtype)], axis=0)
        return ref[rows, :]

    z = _dot3(load(sm_ref, pl.ds(0, t_rows)), wah_ref[...], wal_ref[...]) + ba_ref[...]
    la_all = (jnp.minimum(z, 0.0) - jnp.log1p(jnp.exp(-jnp.abs(z)))) / GLA_TAU
    if padded:
        row_id = lax.broadcasted_iota(jnp.int32, la_all.shape, 0)
        la_all = jnp.where(row_id < t_real, la_all, 0.0)
    rr = lax.broadcasted_iota(jnp.int32, (t_rows, t_rows), 0)
    cc = lax.broadcasted_iota(jnp.int32, (t_rows, t_rows), 1)
    tri = jnp.where((rr >= cc) & (jnp.right_shift(rr, 6) == jnp.right_shift(cc, 6)), 1.0, 0.0).astype(BF16)
    la_hi, la_lo = _split(la_all)
    cum_all = _dot(tri, la_hi) + _dot(tri, la_lo)

    for c in range(n_chunks):
        rows = pl.ds(c * c_len, c_len)
        qk = load(qk_ref, rows)
        v = load(v_ref, rows)
        cum = cum_all[c * c_len:(c + 1) * c_len]
        last = cum[c_len - 1:c_len, :]
        e_q = jnp.exp(cum)
        e_k = jnp.exp(-cum)
        e_kd = jnp.exp(last - cum)
        e_l = jnp.exp(last)
        for h in range(GLA_HEADS):
            sl = slice(h * GLA_DK, (h + 1) * GLA_DK)
            qh = qk[:, sl] * (GLA_DK ** -0.5)
            kh = qk[:, GLA_HEADS * GLA_DK + h * GLA_DK:GLA_HEADS * GLA_DK + (h + 1) * GLA_DK]
            vh = v[:, h * GLA_DV:(h + 1) * GLA_DV].astype(BF16)
            q_dec = (qh * e_q[:, sl]).astype(BF16)
            k_inv = (kh * e_k[:, sl]).astype(BF16)
            k_dec = (kh * e_kd[:, sl]).astype(BF16)
            att = jnp.where(causal, _dot_nt(q_dec, k_inv), 0.0).astype(BF16)
            s_t = st_ref[h]
            o = _dot(att, vh) + _dot_nt(q_dec, s_t.astype(BF16))
            st_ref[h] = s_t * e_l[:, sl] + _dot_tn(vh, k_dec)
            gate = _silu(load(og_ref, rows)[:, h * GLA_DV:(h + 1) * GLA_DV])
            on = o * lax.rsqrt(jnp.mean(o * o, axis=-1, keepdims=True) + RMS_EPS) * gn_ref[...]
            res = on * gate
            if padded:
                o_ref[:, h * GLA_DV:(h + 1) * GLA_DV] = res[:t_real]
            else:
                o_ref[rows, h * GLA_DV:(h + 1) * GLA_DV] = res

    @pl.when(t == pl.num_programs(1) - 1)
    def _():
        for h in range(GLA_HEADS):
            sf_ref[0, h] = st_ref[h].T


def _gla(qk, v, og, sm, wa_hi, wa_lo, ba, gn, s0, nb, t_len):
    n = nb * t_len
    if t_len >= GLA_CHUNK:
        tt = min(t_len, 512)
        t_real = GLA_CHUNK
        n_chunks = tt // GLA_CHUNK
    else:
        tt = t_len
        t_real = t_len
        n_chunks = 1
    nt = t_len // tt
    row = lambda b, t: (b * nt + t, 0)
    const = lambda b, t: (0, 0)
    in_specs = [pl.BlockSpec((tt, 512), row), pl.BlockSpec((tt, 512), row), pl.BlockSpec((tt, 512), row),
                pl.BlockSpec((tt, LANES), row),
                pl.BlockSpec((LANES, 256), const), pl.BlockSpec((LANES, 256), const),
                pl.BlockSpec((1, 256), const), pl.BlockSpec((1, GLA_DV), const)]
    args = [qk, v, og, sm, wa_hi, wa_lo, ba, gn]
    if s0 is not None:
        in_specs.append(pl.BlockSpec((1, GLA_HEADS, GLA_DK, GLA_DV), lambda b, t: (b, 0, 0, 0)))
        args.append(s0)
    return pl.pallas_call(
        functools.partial(_gla_body, t_real=t_real, n_chunks=n_chunks, has_s0=s0 is not None),
        grid=(nb, nt),
        in_specs=in_specs,
        out_specs=[pl.BlockSpec((tt, 512), row),
                   pl.BlockSpec((1, GLA_HEADS, GLA_DK, GLA_DV), lambda b, t: (b, 0, 0, 0))],
        out_shape=[jax.ShapeDtypeStruct((n, 512), F32),
                   jax.ShapeDtypeStruct((nb, GLA_HEADS, GLA_DK, GLA_DV), F32)],
        scratch_shapes=[pltpu.VMEM((GLA_HEADS, GLA_DV, GLA_DK), F32)],
        compiler_params=_cparams(("parallel", "arbitrary")),
        name="gla",
    )(*args)


def _gelu_tanh(x):
    return 0.5 * x * (1.0 + jnp.tanh(0.7978845608028654 * (x + 0.044715 * x * x * x)))


def _compress_body(*refs, n_pages, rows_per_step, n_groups):
    if n_pages:
        pages = refs[1:1 + n_pages]
        wk_ref, wv_ref, pe_ref, w1h_ref, w1l_ref, w2_ref, out_ref, p_ref, rk_ref, rv_ref = refs[1 + n_pages:]
        for idx, page in enumerate(pages):
            rk_ref[idx * PAGE_SIZE:(idx + 1) * PAGE_SIZE, :] = page[0, 0:LANES, :].T
            rv_ref[idx * PAGE_SIZE:(idx + 1) * PAGE_SIZE, :] = page[0, LANES:2 * LANES, :].T
    else:
        rk_ref, rv_ref, wk_ref, wv_ref, pe_ref, w1h_ref, w1l_ref, w2_ref, out_ref, p_ref = refs
    j = pl.program_id(1)
    g_step = rows_per_step // CMP_STRIDE

    def gather(r):
        cols = [r[pl.ds(l, g_step, stride=CMP_STRIDE), :] for l in range(CMP_STRIDE)]
        return jnp.concatenate(cols, axis=1).astype(BF16)

    rows = pl.ds(pl.multiple_of(j * g_step, g_step), g_step)
    p_ref[rows, 0:512] = _dot(gather(rk_ref), wk_ref[...])
    p_ref[rows, 512:1024] = _dot(gather(rv_ref), wv_ref[...])

    @pl.when(j == pl.num_programs(1) - 1)
    def _():
        for s in range(2):
            p = p_ref[:, s * 512:(s + 1) * 512]
            pe = jnp.broadcast_to(pe_ref[s:s + 1, :], (8, CMP_LEN * NSA_HD))
            bias = _dot3(pe, w1h_ref[s], w1l_ref[s])[0:1, :]
            bias2 = jnp.concatenate([bias, bias], axis=1)
            h = p[:, 0:256] + pltpu.roll(p[:, 256:512], n_groups - 1, 0) + bias2
            out_ref[0, :, s * LANES:(s + 1) * LANES] = _dot(_gelu_tanh(h).astype(BF16), w2_ref[s]).astype(BF16)


def _compress_specs_tail():
    c2 = lambda b, j: (0, 0)
    c3 = lambda b, j: (0, 0, 0)
    return [pl.BlockSpec((16 * LANES, 512), c2), pl.BlockSpec((16 * LANES, 512), c2),
            pl.BlockSpec((2, CMP_LEN * NSA_HD), c2),
            pl.BlockSpec((2, CMP_LEN * NSA_HD, CMP_HIDDEN), c3),
            pl.BlockSpec((2, CMP_LEN * NSA_HD, CMP_HIDDEN), c3),
            pl.BlockSpec((2, 2 * CMP_HIDDEN, LANES), c3)]


def _compress_prompt(rows4, cw, nb, t_len):
    chunk = min(t_len, 2048)
    nj = t_len // chunk
    n_groups = t_len // CMP_STRIDE
    return pl.pallas_call(
        functools.partial(_compress_body, n_pages=0, rows_per_step=chunk, n_groups=n_groups),
        grid=(nb, nj),
        in_specs=[pl.BlockSpec((chunk, LANES), lambda b, j: (b * nj + j, 0)),
                  pl.BlockSpec((chunk, LANES), lambda b, j: (b * nj + j, 1))] + _compress_specs_tail(),
        out_specs=pl.BlockSpec((1, n_groups, 256), lambda b, j: (b, 0, 0)),
        out_shape=jax.ShapeDtypeStruct((nb, n_groups, 256), BF16),
        scratch_shapes=[pltpu.VMEM((n_groups, 1024), F32)],
        compiler_params=_cparams(("parallel", "arbitrary")),
        name="compress_prompt",
    )(rows4, rows4, *cw)


def _compress_sample(cache_t, page_table, cw, nb, n_pages):
    pages_per_step = min(n_pages, 16)
    nj = n_pages // pages_per_step
    n_groups = n_pages * PAGE_SIZE // CMP_STRIDE
    rows_per_step = pages_per_step * PAGE_SIZE

    def page_spec(k):
        return pl.BlockSpec((1, 2 * LANES, PAGE_SIZE), lambda b, j, pt: (pt[b, j * pages_per_step + k], 0, 0))

    tail = _compress_specs_tail()
    tail = [pl.BlockSpec(s.block_shape, (lambda f: (lambda b, j, pt: f(b, j)))(s.index_map)) for s in tail]
    grid_spec = pltpu.PrefetchScalarGridSpec(
        num_scalar_prefetch=1,
        grid=(nb, nj),
        in_specs=[page_spec(k) for k in range(pages_per_step)] + tail,
        out_specs=pl.BlockSpec((1, n_groups, 256), lambda b, j, pt: (b, 0, 0)),
        scratch_shapes=[pltpu.VMEM((n_groups, 1024), F32), pltpu.VMEM((rows_per_step, LANES), F32),
                        pltpu.VMEM((rows_per_step, LANES), F32)],
    )
    return pl.pallas_call(
        functools.partial(_compress_body, n_pages=pages_per_step, rows_per_step=rows_per_step, n_groups=n_groups),
        grid_spec=grid_spec,
        out_shape=jax.ShapeDtypeStruct((nb, n_groups, 256), BF16),
        compiler_params=_cparams(("parallel", "arbitrary")),
        name="compress_sample",
    )(page_table, *([cache_t] * pages_per_step), *cw)


def _rep_rows(x, n):
    return jnp.concatenate([x] * n, axis=0)


def _qbd(nq, qb):
    zero = jnp.zeros((NSA_HPG * qb, NSA_HD), F32)
    blocks = []
    for g in range(NSA_GROUPS):
        parts = [nq[:, (g * NSA_HPG + h) * NSA_HD:(g * NSA_HPG + h + 1) * NSA_HD] for h in range(NSA_HPG)]
        qs = jnp.concatenate(parts, axis=0) * (NSA_HD ** -0.5 * LOG2E)
        blocks.append(jnp.concatenate([qs, zero] if g == 0 else [zero, qs], axis=1))
    return jnp.concatenate(blocks, axis=0).astype(BF16)


def _compressed_branch(qbd, kc, vc, pos_c, qb):
    nc = kc.shape[0]
    cmp_end = lax.broadcasted_iota(jnp.int32, (qb, nc), 1) * CMP_STRIDE + (CMP_LEN - 1)
    valid = jnp.where(cmp_end <= pos_c, 1.0, 0.0)
    s = _dot_nt(qbd, kc) + _rep_rows(jnp.where(valid > 0.5, 0.0, NEG_BIG), NSA_HEADS)
    e = jnp.exp2(s - jnp.max(s, axis=-1, keepdims=True)) * _rep_rows(valid, NSA_HEADS)
    p = e / jnp.maximum(jnp.sum(e, axis=-1, keepdims=True), 1e-30)
    return _dot(p.astype(BF16), vc), p


def _head_sum(p, g, qb):
    r0 = g * NSA_HPG * qb
    out = p[r0:r0 + qb]
    for h in range(1, NSA_HPG):
        out = out + p[r0 + h * qb:r0 + (h + 1) * qb]
    return out


def _select_blocks(imp_t, pos_r, n_pick, n_blocks):
    shape = imp_t.shape
    s_io = lax.broadcasted_iota(jnp.int32, shape, 0)
    cur = jnp.right_shift(pos_r, 6)
    valid = (s_io <= cur) & (s_io < n_blocks)
    forced = (s_io == 0) | (s_io == cur) | (s_io == cur - 1)
    bits = pltpu.bitcast(imp_t, jnp.int32)
    packed = pltpu.bitcast((bits & jnp.int32(-N_SLC_PAD)) | (N_SLC_PAD - 1 - s_io), F32)
    key = jnp.where(valid & jnp.logical_not(forced), packed, -1.0)
    sel = jnp.where(valid & forced, 1.0, 0.0)
    for _ in range(n_pick):
        m = jnp.max(key, axis=0, keepdims=True)
        pick = (key == m) & (m >= 0.0)
        sel = jnp.where(pick, 1.0, sel)
        key = jnp.where(pick, -1.0, key)
    return sel


def _unselected(p_sum, agg_t, pos_r, n_pick):
    hi, lo = _split(p_sum)
    imp_t = _dot_nt(agg_t, hi) + _dot_nt(agg_t, lo)
    n_blocks = p_sum.shape[1] * CMP_STRIDE // SLC_LEN
    sel_t = _select_blocks(imp_t, pos_r, n_pick, n_blocks)
    return (1.0 - sel_t.T).astype(BF16)


def _online_update(s, v, m, l, acc, v_transposed=False):
    m_new = jnp.maximum(m, jnp.max(s, axis=-1, keepdims=True))
    alpha = jnp.exp2(m - m_new)
    p = jnp.exp2(s - m_new)
    l_new = alpha * l + jnp.sum(p, axis=-1, keepdims=True)
    pv = _dot_nt(p.astype(BF16), v) if v_transposed else _dot(p.astype(BF16), v)
    return m_new, l_new, alpha * acc + pv


def _softmax_rows(s):
    e = jnp.exp2(s - jnp.max(s, axis=-1, keepdims=True))
    return e / jnp.sum(e, axis=-1, keepdims=True)


def _combine_and_store(o_ref, gsig, o_c, o_s, o_w, nn, qb):
    rows = NSA_HPG * qb
    for g in range(NSA_GROUPS):
        def gcol(j):
            cols = [gsig[:, SM_GATE_OFF + (g * NSA_HPG + h) * 3 + j:SM_GATE_OFF + (g * NSA_HPG + h) * 3 + j + 1]
                    for h in range(NSA_HPG)]
            return jnp.concatenate(cols, axis=0)

        def blk(x):
            return x[g * rows:(g + 1) * rows, g * NSA_HD:(g + 1) * NSA_HD]

        o = gcol(0) * blk(o_c) + gcol(1) * blk(o_s) + gcol(2) * blk(o_w)
        o = o * lax.rsqrt(jnp.mean(o * o, axis=-1, keepdims=True) + RMS_EPS) * nn
        for h in range(NSA_HPG):
            c0 = (g * NSA_HPG + h) * NSA_HD
            o_ref[:, c0:c0 + NSA_HD] = o[h * qb:(h + 1) * qb]


def _rep_lanes(x, n):
    return jnp.concatenate([x] * n, axis=1)


def _softmax_cols(s):
    e = jnp.exp2(s - jnp.max(s, axis=0, keepdims=True))
    return e / jnp.sum(e, axis=0, keepdims=True)


def _nsa_prompt_body(nq_ref, sm_ref, kcv_ref, kvb_ref, ebt_ref, agg_ref, gb_ref, nn_ref, o_ref, *, qb):
    i = pl.program_id(1)
    s0 = i * qb
    pos_r = s0 + lax.broadcasted_iota(jnp.int32, (1, qb), 1)
    cols = NSA_HEADS * qb
    nq_t = (nq_ref[...] * (NSA_HD ** -0.5 * LOG2E)).T
    zero = jnp.zeros((NSA_HD, qb), F32)
    q_rows = []
    for g in range(NSA_GROUPS):
        blks = [nq_t[gh * NSA_HD:(gh + 1) * NSA_HD, :] if gh // NSA_HPG == g else zero for gh in range(NSA_HEADS)]
        q_rows.append(jnp.concatenate(blks, axis=1))
    qbd_t = jnp.concatenate(q_rows, axis=0).astype(BF16)

    nc = kcv_ref.shape[1]
    cmp_end = lax.broadcasted_iota(jnp.int32, (nc, qb), 0) * CMP_STRIDE + (CMP_LEN - 1)
    valid = jnp.where(cmp_end <= pos_r, 1.0, 0.0)
    s_c = _dot(kcv_ref[0, :, 0:LANES], qbd_t) + _rep_lanes(jnp.where(valid > 0.5, 0.0, NEG_BIG), NSA_HEADS)
    e_c = jnp.exp2(s_c - jnp.max(s_c, axis=0, keepdims=True)) * _rep_lanes(valid, NSA_HEADS)
    p_c = e_c / jnp.maximum(jnp.sum(e_c, axis=0, keepdims=True), 1e-30)
    o_c = _dot_tn(kcv_ref[0, :, LANES:2 * LANES], p_c.astype(BF16))

    unsel = []
    for g in range(NSA_GROUPS):
        p_sum = p_c[:, g * NSA_HPG * qb:(g * NSA_HPG + 1) * qb]
        for h in range(1, NSA_HPG):
            p_sum = p_sum + p_c[:, (g * NSA_HPG + h) * qb:(g * NSA_HPG + h + 1) * qb]
        hi, lo = _split(p_sum)
        imp_t = _dot(agg_ref[...], hi) + _dot(agg_ref[...], lo)
        sel_t = _select_blocks(imp_t, pos_r, SLC_TOP - 3, nc * CMP_STRIDE // SLC_LEN)
        unsel += [(1.0 - sel_t).astype(BF16)] * NSA_HPG
    lhs_t = jnp.concatenate([qbd_t, jnp.concatenate(unsel, axis=1)], axis=0)

    def scores(kt):
        krows = pl.ds(pl.multiple_of(kt * KV_TILE, KV_TILE), KV_TILE)
        rhs = jnp.concatenate([kvb_ref[0, krows, 0:LANES], ebt_ref[krows, :]], axis=1)
        return _dot(rhs, lhs_t)

    def causal_bias(kt):
        key_pos = kt * KV_TILE + lax.broadcasted_iota(jnp.int32, (KV_TILE, qb), 0)
        return _rep_lanes(jnp.where(key_pos <= pos_r, 0.0, NEG_BIG), NSA_HEADS)

    def update(kt, s, carry):
        krows = pl.ds(pl.multiple_of(kt * KV_TILE, KV_TILE), KV_TILE)
        m, l, acc = carry
        m_new = jnp.maximum(m, jnp.max(s, axis=0, keepdims=True))
        alpha = jnp.exp2(m - m_new)
        p = jnp.exp2(s - m_new)
        l_new = alpha * l + jnp.sum(p, axis=0, keepdims=True)
        acc_new = alpha * acc + _dot_tn(kvb_ref[0, krows, LANES:2 * LANES], p.astype(BF16))
        return m_new, l_new, acc_new

    def pair(kp, carry):
        s_a = scores(2 * kp)
        s_b = scores(2 * kp + 1)
        return update(2 * kp + 1, s_b, update(2 * kp, s_a, carry))

    n_tiles = (s0 + qb - 1) // KV_TILE + 1
    n_pairs = (n_tiles - 1) // 2
    last = kvb_ref.shape[1] // KV_TILE - 1
    init = (jnp.full((1, cols), NEG_BIG, F32), jnp.zeros((1, cols), F32), jnp.zeros((LANES, cols), F32))
    carry = lax.fori_loop(0, n_pairs, pair, init)
    kt_a = 2 * n_pairs
    kt_b = jnp.minimum(kt_a + 1, last)
    s_a = scores(kt_a) + causal_bias(kt_a)
    s_b = scores(kt_b) + causal_bias(kt_a + 1)
    _, l, acc = update(kt_b, s_b, update(kt_a, s_a, carry))
    o_s = acc / l

    w_len = WINDOW + qb
    w_start = pl.multiple_of(jnp.maximum(s0 - WINDOW, 0), qb)
    wrows = pl.ds(w_start, w_len)
    pos_w = w_start + lax.broadcasted_iota(jnp.int32, (w_len, qb), 0)
    ok = (pos_w <= pos_r) & (pos_r - pos_w < WINDOW)
    s_w = _dot(kvb_ref[0, wrows, 2 * LANES:3 * LANES], qbd_t) + _rep_lanes(jnp.where(ok, 0.0, NEG_BIG), NSA_HEADS)
    o_w = _dot_tn(kvb_ref[0, wrows, 3 * LANES:4 * LANES], _softmax_cols(s_w).astype(BF16))

    gsig_t = _sigmoid(sm_ref[...] + gb_ref[...]).T
    for pair in range(NSA_HEADS // 2):
        blocks = []
        for hh in (2 * pair, 2 * pair + 1):
            g = hh // NSA_HPG

            def blk(x):
                return x[g * NSA_HD:(g + 1) * NSA_HD, hh * qb:(hh + 1) * qb]

            def gate(j):
                r = SM_GATE_OFF + hh * 3 + j
                return gsig_t[r:r + 1, :]

            o = gate(0) * blk(o_c) + gate(1) * blk(o_s) + gate(2) * blk(o_w)
            blocks.append(o * lax.rsqrt(jnp.mean(o * o, axis=0, keepdims=True) + RMS_EPS) * nn_ref[...])
        o_ref[:, pair * LANES:(pair + 1) * LANES] = jnp.concatenate(blocks, axis=0).T


def _nsa_prompt(nq, sm, kcv, kvb3, ebt, agg_t, gate_b, nsa_norm, nb, t_len):
    qb = 128
    nqb = t_len // qb
    nc = t_len // CMP_STRIDE
    row = lambda b, i: (b * nqb + i, 0)
    return pl.pallas_call(
        functools.partial(_nsa_prompt_body, qb=qb),
        grid=(nb, nqb),
        in_specs=[pl.BlockSpec((qb, 512), row), pl.BlockSpec((qb, LANES), row),
                  pl.BlockSpec((1, nc, 256), lambda b, i: (b, 0, 0)),
                  pl.BlockSpec((1, t_len, 512), lambda b, i: (b, 0, 0)),
                  pl.BlockSpec((t_len, N_SLC_PAD), lambda b, i: (0, 0)),
                  pl.BlockSpec((N_SLC_PAD, nc), lambda b, i: (0, 0)),
                  pl.BlockSpec((1, LANES), lambda b, i: (0, 0)),
                  pl.BlockSpec((NSA_HD, 1), lambda b, i: (0, 0))],
        out_specs=pl.BlockSpec((qb, 512), row),
        out_shape=jax.ShapeDtypeStruct((nb * t_len, 512), F32),
        compiler_params=_cparams(("parallel", "arbitrary")),
        name="nsa_prompt",
    )(nq, sm, kcv, kvb3, ebt, agg_t, gate_b, nsa_norm)


def _nsa_sample_body(*refs, n_new, past_len, pages_per_tile, w_buf):
    pages = refs[1:1 + pages_per_tile]
    (nq_ref, sm_ref, kcv_ref, new_ref, wnew_ref, win_ref, eb_ref, agg_ref, gb_ref, nn_ref, o_ref,
     lhs_ref, oc_ref, m_ref, l_ref, acc_ref) = refs[1 + pages_per_tile:]
    kt = pl.program_id(1)
    qb = n_new
    rows = NSA_HEADS * qb
    pos_c = past_len + lax.broadcasted_iota(jnp.int32, (qb, 1), 0)

    @pl.when(kt == 0)
    def _():
        qbd = _qbd(nq_ref[...], qb)
        o_c, p_c = _compressed_branch(qbd, kcv_ref[0, :, 0:LANES], kcv_ref[0, :, LANES:2 * LANES], pos_c, qb)
        oc_ref[...] = o_c
        pos_r = past_len + lax.broadcasted_iota(jnp.int32, (1, N_SLC_PAD), 1)
        unsel = []
        for g in range(NSA_GROUPS):
            p_sum = _head_sum(p_c, g, qb)
            p_pad = jnp.concatenate([p_sum, jnp.zeros((N_SLC_PAD - qb, p_sum.shape[1]), F32)], axis=0)
            unsel += [_unselected(p_pad, agg_ref[...], pos_r, SLC_TOP - 3)[0:qb]] * NSA_HPG
        lhs_ref[...] = jnp.concatenate([qbd, jnp.concatenate(unsel, axis=0)], axis=1)
        m_ref[...] = jnp.full((rows, 1), NEG_BIG, F32)
        l_ref[...] = jnp.zeros((rows, 1), F32)
        acc_ref[...] = jnp.zeros((rows, LANES), F32)

    k_t = jnp.concatenate([p[0, 0:LANES, :] for p in pages], axis=1).astype(BF16)
    v_t = jnp.concatenate([p[0, LANES:2 * LANES, :] for p in pages], axis=1).astype(BF16)
    s = _dot(lhs_ref[...], jnp.concatenate([k_t, eb_ref[0]], axis=0))
    m, l, acc = _online_update(s, v_t, m_ref[...], l_ref[...], acc_ref[...], v_transposed=True)
    m_ref[...] = m
    l_ref[...] = l
    acc_ref[...] = acc

    @pl.when(kt == pl.num_programs(1) - 1)
    def _():
        qbd = lhs_ref[:, 0:LANES]
        pad = jnp.zeros((LANES - n_new, LANES), F32)
        q_idx = lax.broadcasted_iota(jnp.int32, (qb, LANES), 0)
        lane = lax.broadcasted_iota(jnp.int32, (qb, LANES), 1)
        new_ok = _rep_rows(jnp.where((lane < n_new) & (lane <= q_idx), 0.0, NEG_BIG), NSA_HEADS)
        new = new_ref[...]
        k_new = jnp.concatenate([new[:, 2 * LANES:3 * LANES], pad], axis=0).astype(BF16)
        v_new = jnp.concatenate([new[:, 3 * LANES:4 * LANES], pad], axis=0).astype(BF16)
        _, l_f, acc_f = _online_update(_dot_nt(qbd, k_new) + new_ok, v_new, m, l, acc)
        o_s = acc_f / l_f
        pos_w = (past_len - w_buf) + lax.broadcasted_iota(jnp.int32, (1, w_buf), 1)
        ok = (pos_w <= pos_c) & (pos_c - pos_w < WINDOW) & (pos_w >= 0)
        s_w = _dot(qbd, win_ref[0, 0:LANES, :].astype(BF16)) + _rep_rows(jnp.where(ok, 0.0, NEG_BIG), NSA_HEADS)
        wnew = wnew_ref[...]
        kw_new = jnp.concatenate([wnew[:, 0:LANES], pad], axis=0).astype(BF16)
        vw_new = jnp.concatenate([wnew[:, LANES:2 * LANES], pad], axis=0).astype(BF16)
        s_w2 = _dot_nt(qbd, kw_new) + new_ok
        p_w = _softmax_rows(jnp.concatenate([s_w, s_w2], axis=1)).astype(BF16)
        o_w = _dot_nt(p_w[:, 0:w_buf], win_ref[0, LANES:2 * LANES, :].astype(BF16)) + _dot(p_w[:, w_buf:], vw_new)
        gsig = _sigmoid(sm_ref[...] + gb_ref[...])
        _combine_and_store(o_ref, gsig, oc_ref[...], o_s, o_w, nn_ref[...], qb)


def _nsa_sample(nq, sm, kcv, rows_new, win_new, win_t, cache_t, page_table, eb, agg_t, gate_b, nsa_norm,
                nb, n_new, n_pages):
    past_len = n_pages * PAGE_SIZE
    pages_per_tile = min(n_pages, 16)
    n_tiles = n_pages // pages_per_tile
    nc = past_len // CMP_STRIDE
    w_buf = win_t.shape[2]
    rows = NSA_HEADS * n_new

    def page_spec(k):
        return pl.BlockSpec((1, 2 * LANES, PAGE_SIZE), lambda b, t, pt: (pt[b, t * pages_per_tile + k], 1, 0))

    row = lambda b, t, pt: (b, 0)
    const2 = lambda b, t, pt: (0, 0)
    grid_spec = pltpu.PrefetchScalarGridSpec(
        num_scalar_prefetch=1,
        grid=(nb, n_tiles),
        in_specs=[page_spec(k) for k in range(pages_per_tile)] + [
            pl.BlockSpec((n_new, 512), row), pl.BlockSpec((n_new, LANES), row),
            pl.BlockSpec((1, nc, 256), lambda b, t, pt: (b, 0, 0)),
            pl.BlockSpec((n_new, 512), row), pl.BlockSpec((n_new, 256), row),
            pl.BlockSpec((1, 256, w_buf), lambda b, t, pt: (b, 0, 0)),
            pl.BlockSpec((1, N_SLC_PAD, pages_per_tile * PAGE_SIZE), lambda b, t, pt: (t, 0, 0)),
            pl.BlockSpec((N_SLC_PAD, nc), const2),
            pl.BlockSpec((1, LANES), const2), pl.BlockSpec((1, NSA_HD), const2)],
        out_specs=pl.BlockSpec((n_new, 512), row),
        scratch_shapes=[pltpu.VMEM((rows, 2 * LANES), BF16), pltpu.VMEM((rows, LANES), F32),
                        pltpu.VMEM((rows, 1), F32), pltpu.VMEM((rows, 1), F32), pltpu.VMEM((rows, LANES), F32)],
    )
    return pl.pallas_call(
        functools.partial(_nsa_sample_body, n_new=n_new, past_len=past_len, pages_per_tile=pages_per_tile,
                          w_buf=w_buf),
        grid_spec=grid_spec,
        out_shape=jax.ShapeDtypeStruct((nb * n_new, 512), F32),
        compiler_params=_cparams(("parallel", "arbitrary")),
        name="nsa_sample",
    )(page_table, *([cache_t] * pages_per_tile), nq, sm, kcv, rows_new, win_new, win_t, eb, agg_t, gate_b, nsa_norm)


def _route(logits):
    lane = lax.broadcasted_iota(jnp.int32, logits.shape, 1)
    lane_f = lane.astype(F32)
    big = 1e6
    is_g = (lane >= N_EXPERTS) & (lane < N_EXPERTS + N_EXPERT_GROUPS)
    gl = jnp.where(is_g, logits, NEG_BIG)
    gmax = jnp.max(gl, axis=-1, keepdims=True)
    gtop = jnp.min(jnp.where(gl == gmax, lane_f, big), axis=-1, keepdims=True) - N_EXPERTS
    p_group = 1.0 / jnp.sum(jnp.exp(gl - gmax), axis=-1, keepdims=True)
    in_grp = (lane < N_EXPERTS) & (jnp.right_shift(lane, 3).astype(F32) == gtop)
    el = jnp.where(in_grp, logits, NEG_BIG)
    m1 = jnp.max(el, axis=-1, keepdims=True)
    i1 = jnp.min(jnp.where(el == m1, lane_f, big), axis=-1, keepdims=True)
    el2 = jnp.where(lane_f == i1, NEG_BIG, el)
    m2 = jnp.max(el2, axis=-1, keepdims=True)
    i2 = jnp.min(jnp.where(el2 == m2, lane_f, big), axis=-1, keepdims=True)
    r = jnp.exp(m2 - m1)
    t1 = 1.0 / (1.0 + r)
    t2 = r * t1
    return jnp.where(lane_f == i1, p_group * t1, 0.0) + jnp.where(lane_f == i2, p_group * t2, 0.0)


def _mix_body(og_ref, on_ref, x_ref, wo_ref, g_ref, b_ref, rwh_ref, rwl_ref, rb_ref, x1_ref, cmb_ref):
    mix = _dot(og_ref[...].astype(BF16), wo_ref[0:512, :]) + _dot(on_ref[...].astype(BF16), wo_ref[512:1024, :])
    x1 = _layer_norm(DEEPNORM_ALPHA * x_ref[...] + mix, g_ref[...], b_ref[...])
    x1_ref[...] = x1
    logits = _dot3(x1, rwh_ref[...], rwl_ref[...]) + rb_ref[...]
    cmb_ref[...] = _route(logits)


def _mix(o_gla, o_nsa, x2d, wo, g, b, rw_hi, rw_lo, rb):
    n = x2d.shape[0]
    tm = min(n, 512)
    row = lambda i: (i, 0)
    const = lambda i: (0, 0)
    return pl.pallas_call(
        _mix_body,
        grid=(n // tm,),
        in_specs=[pl.BlockSpec((tm, 512), row), pl.BlockSpec((tm, 512), row), pl.BlockSpec((tm, D_MODEL), row),
                  pl.BlockSpec((D_MODEL, D_MODEL), const), pl.BlockSpec((1, D_MODEL), const),
                  pl.BlockSpec((1, D_MODEL), const), pl.BlockSpec((D_MODEL, LANES), const),
                  pl.BlockSpec((D_MODEL, LANES), const), pl.BlockSpec((1, LANES), const)],
        out_specs=[pl.BlockSpec((tm, D_MODEL), row), pl.BlockSpec((tm, LANES), row)],
        out_shape=[jax.ShapeDtypeStruct((n, D_MODEL), F32), jax.ShapeDtypeStruct((n, LANES), F32)],
        compiler_params=_cparams(("parallel",)),
        name="mix",
    )(o_gla, o_nsa, x2d, wo, g, b, rw_hi, rw_lo, rb)


def _moe_body(x1_ref, cmb_ref, tri_ref, wg_ref, wu_ref, wd_ref, g_ref, b_ref, o_ref,
              acc_ref, xb_ref, key_ref, keyt_ref, cmbt_ref, *, chunk):
    j = pl.program_id(1)
    tm = x1_ref.shape[0]

    @pl.when(j == 0)
    def _():
        acc_ref[...] = jnp.zeros_like(acc_ref)
        xb_ref[...] = x1_ref[...].astype(BF16)
        cmb = cmb_ref[...]
        routed = cmb != 0.0
        rank = _dot(tri_ref[...], jnp.where(routed, 1.0, 0.0).astype(BF16))
        key = jnp.where(routed, rank, -1.0)
        key_ref[...] = key
        keyt_ref[...] = key.T
        cmbt_ref[...] = cmb.T

    lane = lax.broadcasted_iota(jnp.int32, (tm, LANES), 1)
    r_io = lax.broadcasted_iota(jnp.int32, (chunk, tm), 0).astype(F32)
    c_io = lax.broadcasted_iota(jnp.int32, (tm, chunk), 1).astype(F32)
    key_rows, w_rows, key_cols, counts = [], [], [], []
    for k in range(MOE_PAIR):
        e = MOE_PAIR * j + k
        key_rows.append(keyt_ref[pl.ds(e, 1), :])
        w_rows.append(cmbt_ref[pl.ds(e, 1), :])
        key_cols.append(jnp.sum(jnp.where(lane == e, key_ref[...], 0.0), axis=-1, keepdims=True))
        counts.append(jnp.sum(jnp.where(key_rows[k] >= 0.0, 1.0, 0.0)).astype(jnp.int32))

    def one_chunk(c, carry):
        base = (c * chunk).astype(F32)
        sels = [jnp.where(key_rows[k] - base == r_io, 1.0, 0.0) for k in range(MOE_PAIR)]
        xe = _dot(jnp.concatenate(sels, axis=0).astype(BF16), xb_ref[...]).astype(BF16)
        ys = []
        for k in range(MOE_PAIR):
            xk = xe[k * chunk:(k + 1) * chunk]
            hdn = _silu(_dot(xk, wg_ref[k])) * _dot(xk, wu_ref[k])
            wc = jnp.sum(sels[k] * w_rows[k], axis=-1, keepdims=True)
            ys.append((wc * _dot(hdn.astype(BF16), wd_ref[k])).astype(BF16))
        sel_t = jnp.concatenate([jnp.where(key_cols[k] - base == c_io, 1.0, 0.0) for k in range(MOE_PAIR)],
                                axis=1).astype(BF16)
        acc_ref[...] += _dot(sel_t, jnp.concatenate(ys, axis=0))
        return carry

    n_chunks = (functools.reduce(jnp.maximum, counts) + chunk - 1) // chunk
    lax.fori_loop(0, n_chunks, one_chunk, 0)

    @pl.when(j == pl.num_programs(1) - 1)
    def _():
        o_ref[...] = _layer_norm(DEEPNORM_ALPHA * x1_ref[...] + acc_ref[...], g_ref[...], b_ref[...])


def _moe(x1, cmb, wg, wu, wd, g, b):
    n = x1.shape[0]
    tm = min(n, MOE_TILE)
    chunk = min(tm, MOE_CHUNK)
    tri = jnp.where(jnp.arange(tm)[:, None] > jnp.arange(tm)[None, :], 1.0, 0.0).astype(BF16)
    row = lambda i, j: (i, 0)
    const = lambda i, j: (0, 0)
    return pl.pallas_call(
        functools.partial(_moe_body, chunk=chunk),
        grid=(n // tm, N_EXPERTS // MOE_PAIR),
        in_specs=[pl.BlockSpec((tm, D_MODEL), row), pl.BlockSpec((tm, LANES), row),
                  pl.BlockSpec((tm, tm), const),
                  pl.BlockSpec((MOE_PAIR, D_MODEL, D_EXPERT), lambda i, j: (j, 0, 0)),
                  pl.BlockSpec((MOE_PAIR, D_MODEL, D_EXPERT), lambda i, j: (j, 0, 0)),
                  pl.BlockSpec((MOE_PAIR, D_EXPERT, D_MODEL), lambda i, j: (j, 0, 0)),
                  pl.BlockSpec((1, D_MODEL), const), pl.BlockSpec((1, D_MODEL), const)],
        out_specs=pl.BlockSpec((tm, D_MODEL), row),
        out_shape=jax.ShapeDtypeStruct((n, D_MODEL), F32),
        scratch_shapes=[pltpu.VMEM((tm, D_MODEL), F32), pltpu.VMEM((tm, D_MODEL), BF16),
                        pltpu.VMEM((tm, LANES), F32), pltpu.VMEM((LANES, tm), F32), pltpu.VMEM((LANES, tm), F32)],
        compiler_params=_cparams(("parallel", "arbitrary")),
        name="moe",
    )(x1, cmb, tri, wg, wu, wd, g, b)


def _prep_weights(w_in, gla_w_a, gla_b_a, nsa_gate_b, cmp_pe, cmp_w1, cmp_w2, gla_norm, nsa_norm, w_o,
                  ln1_g, ln1_b, router_g_w, router_g_b, router_e_w, router_e_b, w_gate, w_up, w_down,
                  ln2_g, ln2_b, n_cmp):
    o_glr, o_gog, o_nq, o_nkv, o_ngt = 1024, 1040, 1552, 2064, 2832
    pad = jnp.zeros((D_MODEL, LANES - GLA_RANK - 3 * NSA_HEADS), F32)
    w_proj = jnp.concatenate([w_in[:, 0:512], w_in[:, 512:1024], w_in[:, o_gog:o_gog + 512],
                              w_in[:, o_nq:o_nq + 512], w_in[:, o_nkv:o_nkv + 768],
                              w_in[:, o_glr:o_glr + GLA_RANK], w_in[:, o_ngt:o_ngt + 24], pad], axis=1).astype(BF16)
    wa = jnp.zeros((LANES, GLA_HEADS * GLA_DK), F32).at[0:GLA_RANK].set(gla_w_a)
    wa_hi, wa_lo = _split(wa)
    ba = gla_b_a.reshape(1, -1)
    gate_b = jnp.zeros((1, LANES), F32).at[0, SM_GATE_OFF:SM_GATE_OFF + 24].set(nsa_gate_b)
    eye = jnp.eye(NSA_GROUPS, dtype=F32)
    w1r = cmp_w1.reshape(2, CMP_LEN, NSA_HD, CMP_HIDDEN)

    def split_w1(w):
        wa_ = jnp.einsum('ldj,gh->lgdhj', w[:CMP_STRIDE], eye).reshape(CMP_STRIDE * LANES, 2 * CMP_HIDDEN)
        wb_ = jnp.einsum('ldj,gh->lgdhj', w[CMP_STRIDE:], eye).reshape(CMP_STRIDE * LANES, 2 * CMP_HIDDEN)
        return jnp.concatenate([wa_, wb_], axis=1).astype(BF16)

    wk, wv = split_w1(w1r[0]), split_w1(w1r[1])
    pe = cmp_pe.reshape(2, CMP_LEN * NSA_HD)
    w1_hi, w1_lo = _split(cmp_w1)
    w2bd = jnp.einsum('sjd,gh->sgjhd', cmp_w2, eye).reshape(2, 2 * CMP_HIDDEN, LANES).astype(BF16)
    cw = (wk, wv, pe, w1_hi, w1_lo, w2bd)
    starts = np.arange(n_cmp) * CMP_STRIDE
    sel_starts = np.arange(N_SLC_PAD) * SLC_LEN
    agg_t = ((starts[None, :] < sel_starts[:, None] + SLC_LEN)
             & (starts[None, :] + CMP_LEN > sel_starts[:, None])
             & (np.arange(n_cmp)[None, :] < n_cmp - 1))
    agg_t = jnp.asarray(agg_t, BF16)
    rw = jnp.zeros((D_MODEL, LANES), F32)
    rw = rw.at[:, 0:N_EXPERTS].set(router_e_w.transpose(1, 0, 2).reshape(D_MODEL, N_EXPERTS))
    rw = rw.at[:, N_EXPERTS:N_EXPERTS + N_EXPERT_GROUPS].set(router_g_w)
    rw_hi, rw_lo = _split(rw)
    rb = jnp.zeros((1, LANES), F32).at[0, 0:N_EXPERTS].set(router_e_b.reshape(-1))
    rb = rb.at[0, N_EXPERTS:N_EXPERTS + N_EXPERT_GROUPS].set(router_g_b)
    return dict(w_proj=w_proj, wa_hi=wa_hi, wa_lo=wa_lo, ba=ba, gate_b=gate_b, cw=cw, agg_t=agg_t,
                gn=gla_norm.reshape(1, -1), nn=nsa_norm.reshape(1, -1), wo=w_o.astype(BF16),
                ln1_g=ln1_g.reshape(1, -1), ln1_b=ln1_b.reshape(1, -1), rw_hi=rw_hi, rw_lo=rw_lo, rb=rb,
                wg=w_gate.astype(BF16), wu=w_up.astype(BF16), wd=w_down.astype(BF16),
                ln2_g=ln2_g.reshape(1, -1), ln2_b=ln2_b.reshape(1, -1))


def _tail(w, o_gla, o_nsa, x2d):
    x1, cmb = _mix(o_gla, o_nsa, x2d, w['wo'], w['ln1_g'], w['ln1_b'], w['rw_hi'], w['rw_lo'], w['rb'])
    return _moe(x1, cmb, w['wg'], w['wu'], w['wd'], w['ln2_g'], w['ln2_b'])


def kernel(x_prompt, x_sample, cache_nsa, state_win, state_gla, page_table, w_in, gla_w_a, gla_b_a, nsa_gate_b,
           cmp_pe, cmp_w1, cmp_w2, gla_norm, nsa_norm, w_o, ln1_g, ln1_b, router_g_w, router_g_b, router_e_w,
           router_e_b, w_gate, w_up, w_down, ln2_g, ln2_b):
    assert w_in.shape[0] == 1, "single layer"
    bp, tp, _ = x_prompt.shape
    bs, ts, _ = x_sample.shape
    n_pages = page_table.shape[1]
    past_len = n_pages * PAGE_SIZE
    assert tp == past_len, "prompt and past share the compressed-block count"
    w = _prep_weights(w_in[0], gla_w_a[0], gla_b_a[0], nsa_gate_b[0], cmp_pe[0], cmp_w1[0], cmp_w2[0],
                      gla_norm[0], nsa_norm[0], w_o[0], ln1_g[0], ln1_b[0], router_g_w[0], router_g_b[0],
                      router_e_w[0], router_e_b[0], w_gate[0], w_up[0], w_down[0], ln2_g[0], ln2_b[0],
                      n_cmp=tp // CMP_STRIDE)

    xp2 = x_prompt.reshape(bp * tp, D_MODEL)
    qk, v, og, nq, rows4, win, kvb, sm, rows_t = _proj(xp2, w['w_proj'], seq_len=tp)
    o_gla, gla_p = _gla(qk, v, og, sm, w['wa_hi'], w['wa_lo'], w['ba'], w['gn'], None, bp, tp)
    kcv = _compress_prompt(rows4, w['cw'], bp, tp)
    blk_of_key = jnp.arange(tp, dtype=jnp.int32) // SLC_LEN
    ebt = jnp.where(blk_of_key[:, None] == jnp.arange(N_SLC_PAD, dtype=jnp.int32)[None, :], NEG_BIG, 0.0).astype(BF16)
    o_nsa = _nsa_prompt(nq, sm, kcv, kvb.reshape(bp, tp, 512), ebt, w['agg_t'], w['gate_b'], w['nn'].reshape(-1, 1),
                        bp, tp)
    y_p = _tail(w, o_gla, o_nsa, xp2).reshape(bp, tp, D_MODEL)
    w_rows = min(WINDOW, tp)
    rows_p = rows_t.reshape(bp, 4, NSA_GROUPS, NSA_HD, tp).transpose(0, 4, 1, 2, 3)[None]
    win_p = win.reshape(bp, tp, 2, NSA_GROUPS, NSA_HD)[None, :, tp - w_rows:]

    xs2 = x_sample.reshape(bs * ts, D_MODEL)
    qk, v, og, nq, rows4s, win_new, _, sm = _proj(xs2, w['w_proj'])
    o_gla, gla_s = _gla(qk, v, og, sm, w['wa_hi'], w['wa_lo'], w['ba'], w['gn'], state_gla[0], bs, ts)
    cache_t = cache_nsa[0].transpose(0, 2, 3, 4, 1).reshape(-1, 4 * NSA_GROUPS * NSA_HD, PAGE_SIZE)
    w_buf = state_win.shape[2]
    win_t = state_win[0].transpose(0, 2, 3, 4, 1).reshape(bs, 2 * NSA_GROUPS * NSA_HD, w_buf)
    kcv = _compress_sample(cache_t, page_table, w['cw'], bs, n_pages)
    tile_keys = min(n_pages, 16) * PAGE_SIZE
    eb = jnp.where(blk_of_key.reshape(-1, 1, tile_keys) == jnp.arange(N_SLC_PAD, dtype=jnp.int32)[None, :, None],
                   NEG_BIG, 0.0).astype(BF16)
    o_nsa = _nsa_sample(nq, sm, kcv, rows4s, win_new, win_t, cache_t, page_table, eb, w['agg_t'], w['gate_b'],
                        w['nn'], bs, ts, n_pages)
    y_s = _tail(w, o_gla, o_nsa, xs2).reshape(bs, ts, D_MODEL)
    rows_s = rows4s.reshape(1, bs, ts, 4, NSA_GROUPS, NSA_HD)
    win_s = jnp.concatenate([state_win[0][:, ts:], win_new.reshape(bs, ts, 2, NSA_GROUPS, NSA_HD)], axis=1)[None]

    return (y_p, y_s, rows_p, rows_s, win_p, win_s, gla_p[None], gla_s[None])
```

```python
import functools

import numpy as np
import jax
import jax.numpy as jnp
from jax import lax
from jax.experimental import pallas as pl
from jax.experimental.pallas import tpu as pltpu

F32 = jnp.float32
BF16 = jnp.bfloat16

D_MODEL = 1024
GLA_HEADS = 4
GLA_DK = 64
GLA_DV = 128
GLA_RANK = 16
GLA_TAU = 16.0
GLA_CHUNK = 64
NSA_HEADS = 8
NSA_GROUPS = 2
NSA_HPG = 4
NSA_HD = 64
CMP_LEN = 32
CMP_STRIDE = 16
CMP_HIDDEN = 128
SLC_LEN = 64
SLC_TOP = 16
WINDOW = 512
PAGE_SIZE = 128
N_EXPERT_GROUPS = 4
EXPERTS_PER_GROUP = 8
N_EXPERTS = 32
D_EXPERT = 512
DEPTH = 1
DEEPNORM_ALPHA = (2.0 * DEPTH) ** 0.25
LN_EPS = 1e-5
RMS_EPS = 1e-6
NEG_BIG = -1e30
MASKED_MAX = -1e29
LOG2E = 1.4426950408889634
FORCE_SCORE = 1e9

LANES = 128
KV_TILE = 512
BLK_PER_TILE = KV_TILE // SLC_LEN
N_SLC_PAD = 128
VMEM_LIMIT = 56 * 1024 * 1024
MOE_TILE = 1024
MOE_CHUNK = 128
MOE_PAIR = 2

SEG_QK, SEG_V, SEG_OG, SEG_NQ, SEG_ROWS, SEG_WIN, SEG_SM = 0, 512, 1024, 1536, 2048, 2560, 2816
D_PROJ = 2944
SM_GATE_OFF = GLA_RANK


def _cparams(sem):
    return pltpu.CompilerParams(dimension_semantics=sem, vmem_limit_bytes=VMEM_LIMIT)


def _dot(a, b):
    return jnp.dot(a, b, preferred_element_type=F32)


def _dot_nt(a, b):
    return lax.dot_general(a, b, (((1,), (1,)), ((), ())), preferred_element_type=F32)


def _dot_tn(a, b):
    return lax.dot_general(a, b, (((0,), (0,)), ((), ())), preferred_element_type=F32)


def _split(x):
    hi = x.astype(BF16)
    lo = (x - hi.astype(F32)).astype(BF16)
    return hi, lo


def _dot3(a, b_hi, b_lo):
    a_hi, a_lo = _split(a)
    return _dot(a_hi, b_hi) + (_dot(a_hi, b_lo) + _dot(a_lo, b_hi))


def _sigmoid(x):
    return 1.0 / (1.0 + jnp.exp(-x))


def _silu(x):
    return x * _sigmoid(x)


def _layer_norm(y, g, b):
    mu = jnp.mean(y, axis=-1, keepdims=True)
    d = y - mu
    var = jnp.mean(d * d, axis=-1, keepdims=True)
    return d * lax.rsqrt(var + LN_EPS) * g + b


def _proj_body(x_ref, w_ref, qk_ref, v_ref, og_ref, nq_ref, rows_ref, win_ref, kvb_ref, sm_ref, rows_t_ref=None):
    x = x_ref[...].astype(BF16)

    def seg(off, width):
        return _dot(x, w_ref[:, off:off + width])

    qk_ref[...] = seg(SEG_QK, 512)
    v_ref[...] = seg(SEG_V, 512)
    og_ref[...] = seg(SEG_OG, 512)
    nq_ref[...] = seg(SEG_NQ, 512)
    rows = seg(SEG_ROWS, 512)
    rows_ref[...] = rows
    if rows_t_ref is not None:
        rows_t_ref[0] = rows.T
    win = seg(SEG_WIN, 256)
    win_ref[...] = win
    kvb_ref[:, 0:256] = rows[:, 256:512].astype(BF16)
    kvb_ref[:, 256:512] = win.astype(BF16)
    sm_ref[...] = seg(SEG_SM, 128)


def _proj(x2d, w_proj, seq_len=None):
    n = x2d.shape[0]
    tm = min(n, 512)
    widths = (512, 512, 512, 512, 512, 256, 512, 128)
    dtypes = (F32, F32, F32, F32, F32, F32, BF16, F32)
    out_specs = [pl.BlockSpec((tm, w), lambda i: (i, 0)) for w in widths]
    out_shape = [jax.ShapeDtypeStruct((n, w), d) for w, d in zip(widths, dtypes)]
    if seq_len is not None:
        nt = seq_len // tm
        out_specs.append(pl.BlockSpec((1, 512, tm), lambda i: (i // nt, 0, i % nt)))
        out_shape.append(jax.ShapeDtypeStruct((n // seq_len, 512, seq_len), F32))
    return pl.pallas_call(
        _proj_body,
        grid=(n // tm,),
        in_specs=[pl.BlockSpec((tm, D_MODEL), lambda i: (i, 0)),
                  pl.BlockSpec((D_MODEL, D_PROJ), lambda i: (0, 0))],
        out_specs=out_specs,
        out_shape=out_shape,
        compiler_params=_cparams(("parallel",)),
        name="proj",
    )(x2d, w_proj)


def _gla_body(*refs, t_real, n_chunks, has_s0):
    if has_s0:
        qk_ref, v_ref, og_ref, sm_ref, wah_ref, wal_ref, ba_ref, gn_ref, s0_ref, o_ref, sf_ref, st_ref = refs
    else:
        qk_ref, v_ref, og_ref, sm_ref, wah_ref, wal_ref, ba_ref, gn_ref, o_ref, sf_ref, st_ref = refs
        s0_ref = None
    c_len = GLA_CHUNK
    t = pl.program_id(1)

    @pl.when(t == 0)
    def _():
        for h in range(GLA_HEADS):
            if has_s0:
                st_ref[h] = s0_ref[0, h].T
            else:
                st_ref[h] = jnp.zeros((GLA_DV, GLA_DK), F32)

    r_io = lax.broadcasted_iota(jnp.int32, (c_len, c_len), 0)
    c_io = lax.broadcasted_iota(jnp.int32, (c_len, c_len), 1)
    causal = r_io >= c_io
    padded = t_real < c_len
    t_rows = n_chunks * c_len

    def load(ref, rows):
        if padded:
            x = ref[...]
            return jnp.concatenate([x, jnp.zeros((c_len - t_real, x.shape[1]), x.dtype)], axis=0)
        return ref[rows, :]

    z = _dot3(load(sm_ref, pl.ds(0, t_rows)), wah_ref[...], wal_ref[...]) + ba_ref[...]
    la_all = (jnp.minimum(z, 0.0) - jnp.log1p(jnp.exp(-jnp.abs(z)))) / GLA_TAU
    if padded:
        row_id = lax.broadcasted_iota(jnp.int32, la_all.shape, 0)
        la_all = jnp.where(row_id < t_real, la_all, 0.0)
    rr = lax.broadcasted_iota(jnp.int32, (t_rows, t_rows), 0)
    cc = lax.broadcasted_iota(jnp.int32, (t_rows, t_rows), 1)
    tri = jnp.where((rr >= cc) & (jnp.right_shift(rr, 6) == jnp.right_shift(cc, 6)), 1.0, 0.0).astype(BF16)
    la_hi, la_lo = _split(la_all)
    cum_all = _dot(tri, la_hi) + _dot(tri, la_lo)

    for c in range(n_chunks):
        rows = pl.ds(c * c_len, c_len)
        qk = load(qk_ref, rows)
        v = load(v_ref, rows)
        cum = cum_all[c * c_len:(c + 1) * c_len]
        last = cum[c_len - 1:c_len, :]
        e_q = jnp.exp(cum)
        e_k = jnp.exp(-cum)
        e_kd = jnp.exp(last - cum)
        e_l = jnp.exp(last)
        for h in range(GLA_HEADS):
            sl = slice(h * GLA_DK, (h + 1) * GLA_DK)
            qh = qk[:, sl] * (GLA_DK ** -0.5)
            kh = qk[:, GLA_HEADS * GLA_DK + h * GLA_DK:GLA_HEADS * GLA_DK + (h + 1) * GLA_DK]
            vh = v[:, h * GLA_DV:(h + 1) * GLA_DV].astype(BF16)
            q_dec = (qh * e_q[:, sl]).astype(BF16)
            k_inv = (kh * e_k[:, sl]).astype(BF16)
            k_dec = (kh * e_kd[:, sl]).astype(BF16)
            att = jnp.where(causal, _dot_nt(q_dec, k_inv), 0.0).astype(BF16)
            s_t = st_ref[h]
            o = _dot(att, vh) + _dot_nt(q_dec, s_t.astype(BF16))
            st_ref[h] = s_t * e_l[:, sl] + _dot_tn(vh, k_dec)
            gate = _silu(load(og_ref, rows)[:, h * GLA_DV:(h + 1) * GLA_DV])
            on = o * lax.rsqrt(jnp.mean(o * o, axis=-1, keepdims=True) + RMS_EPS) * gn_ref[...]
            res = on * gate
            if padded:
                o_ref[:, h * GLA_DV:(h + 1) * GLA_DV] = res[:t_real]
            else:
                o_ref[rows, h * GLA_DV:(h + 1) * GLA_DV] = res

    @pl.when(t == pl.num_programs(1) - 1)
    def _():
        for h in range(GLA_HEADS):
            sf_ref[0, h] = st_ref[h].T


def _gla(qk, v, og, sm, wa_hi, wa_lo, ba, gn, s0, nb, t_len):
    n = nb * t_len
    if t_len >= GLA_CHUNK:
        tt = min(t_len, 512)
        t_real = GLA_CHUNK
        n_chunks = tt // GLA_CHUNK
    else:
        tt = t_len
        t_real = t_len
        n_chunks = 1
    nt = t_len // tt
    row = lambda b, t: (b * nt + t, 0)
    const = lambda b, t: (0, 0)
    in_specs = [pl.BlockSpec((tt, 512), row), pl.BlockSpec((tt, 512), row), pl.BlockSpec((tt, 512), row),
                pl.BlockSpec((tt, LANES), row),
                pl.BlockSpec((LANES, 256), const), pl.BlockSpec((LANES, 256), const),
                pl.BlockSpec((1, 256), const), pl.BlockSpec((1, GLA_DV), const)]
    args = [qk, v, og, sm, wa_hi, wa_lo, ba, gn]
    if s0 is not None:
        in_specs.append(pl.BlockSpec((1, GLA_HEADS, GLA_DK, GLA_DV), lambda b, t: (b, 0, 0, 0)))
        args.append(s0)
    return pl.pallas_call(
        functools.partial(_gla_body, t_real=t_real, n_chunks=n_chunks, has_s0=s0 is not None),
        grid=(nb, nt),
        in_specs=in_specs,
        out_specs=[pl.BlockSpec((tt, 512), row),
                   pl.BlockSpec((1, GLA_HEADS, GLA_DK, GLA_DV), lambda b, t: (b, 0, 0, 0))],
        out_shape=[jax.ShapeDtypeStruct((n, 512), F32),
                   jax.ShapeDtypeStruct((nb, GLA_HEADS, GLA_DK, GLA_DV), F32)],
        scratch_shapes=[pltpu.VMEM((GLA_HEADS, GLA_DV, GLA_DK), F32)],
        compiler_params=_cparams(("parallel", "arbitrary")),
        name="gla",
    )(*args)


def _gelu_tanh(x):
    return 0.5 * x * (1.0 + jnp.tanh(0.7978845608028654 * (x + 0.044715 * x * x * x)))


def _compress_body(*refs, n_pages, rows_per_step, n_groups):
    if n_pages:
        pages = refs[1:1 + n_pages]
        wk_ref, wv_ref, pe_ref, w1h_ref, w1l_ref, w2_ref, out_ref, p_ref, rk_ref, rv_ref = refs[1 + n_pages:]
        for idx, page in enumerate(pages):
            rk_ref[idx * PAGE_SIZE:(idx + 1) * PAGE_SIZE, :] = page[0, 0:LANES, :].T
            rv_ref[idx * PAGE_SIZE:(idx + 1) * PAGE_SIZE, :] = page[0, LANES:2 * LANES, :].T
    else:
        rk_ref, rv_ref, wk_ref, wv_ref, pe_ref, w1h_ref, w1l_ref, w2_ref, out_ref, p_ref = refs
    j = pl.program_id(1)
    g_step = rows_per_step // CMP_STRIDE

    def gather(r):
        cols = [r[pl.ds(l, g_step, stride=CMP_STRIDE), :] for l in range(CMP_STRIDE)]
        return jnp.concatenate(cols, axis=1).astype(BF16)

    rows = pl.ds(pl.multiple_of(j * g_step, g_step), g_step)
    p_ref[rows, 0:512] = _dot(gather(rk_ref), wk_ref[...])
    p_ref[rows, 512:1024] = _dot(gather(rv_ref), wv_ref[...])

    @pl.when(j == pl.num_programs(1) - 1)
    def _():
        for s in range(2):
            p = p_ref[:, s * 512:(s + 1) * 512]
            pe = jnp.broadcast_to(pe_ref[s:s + 1, :], (8, CMP_LEN * NSA_HD))
            bias = _dot3(pe, w1h_ref[s], w1l_ref[s])[0:1, :]
            bias2 = jnp.concatenate([bias, bias], axis=1)
            h = p[:, 0:256] + pltpu.roll(p[:, 256:512], n_groups - 1, 0) + bias2
            out_ref[0, :, s * LANES:(s + 1) * LANES] = _dot(_gelu_tanh(h).astype(BF16), w2_ref[s]).astype(BF16)


def _compress_specs_tail():
    c2 = lambda b, j: (0, 0)
    c3 = lambda b, j: (0, 0, 0)
    return [pl.BlockSpec((16 * LANES, 512), c2), pl.BlockSpec((16 * LANES, 512), c2),
            pl.BlockSpec((2, CMP_LEN * NSA_HD), c2),
            pl.BlockSpec((2, CMP_LEN * NSA_HD, CMP_HIDDEN), c3),
            pl.BlockSpec((2, CMP_LEN * NSA_HD, CMP_HIDDEN), c3),
            pl.BlockSpec((2, 2 * CMP_HIDDEN, LANES), c3)]


def _compress_prompt(rows4, cw, nb, t_len):
    chunk = min(t_len, 2048)
    nj = t_len // chunk
    n_groups = t_len // CMP_STRIDE
    return pl.pallas_call(
        functools.partial(_compress_body, n_pages=0, rows_per_step=chunk, n_groups=n_groups),
        grid=(nb, nj),
        in_specs=[pl.BlockSpec((chunk, LANES), lambda b, j: (b * nj + j, 0)),
                  pl.BlockSpec((chunk, LANES), lambda b, j: (b * nj + j, 1))] + _compress_specs_tail(),
        out_specs=pl.BlockSpec((1, n_groups, 256), lambda b, j: (b, 0, 0)),
        out_shape=jax.ShapeDtypeStruct((nb, n_groups, 256), BF16),
        scratch_shapes=[pltpu.VMEM((n_groups, 1024), F32)],
        compiler_params=_cparams(("parallel", "arbitrary")),
        name="compress_prompt",
    )(rows4, rows4, *cw)


def _compress_sample(cache_t, page_table, cw, nb, n_pages):
    pages_per_step = min(n_pages, 16)
    nj = n_pages // pages_per_step
    n_groups = n_pages * PAGE_SIZE // CMP_STRIDE
    rows_per_step = pages_per_step * PAGE_SIZE

    def page_spec(k):
        return pl.BlockSpec((1, 2 * LANES, PAGE_SIZE), lambda b, j, pt: (pt[b, j * pages_per_step + k], 0, 0))

    tail = _compress_specs_tail()
    tail = [pl.BlockSpec(s.block_shape, (lambda f: (lambda b, j, pt: f(b, j)))(s.index_map)) for s in tail]
    grid_spec = pltpu.PrefetchScalarGridSpec(
        num_scalar_prefetch=1,
        grid=(nb, nj),
        in_specs=[page_spec(k) for k in range(pages_per_step)] + tail,
        out_specs=pl.BlockSpec((1, n_groups, 256), lambda b, j, pt: (b, 0, 0)),
        scratch_shapes=[pltpu.VMEM((n_groups, 1024), F32), pltpu.VMEM((rows_per_step, LANES), F32),
                        pltpu.VMEM((rows_per_step, LANES), F32)],
    )
    return pl.pallas_call(
        functools.partial(_compress_body, n_pages=pages_per_step, rows_per_step=rows_per_step, n_groups=n_groups),
        grid_spec=grid_spec,
        out_shape=jax.ShapeDtypeStruct((nb, n_groups, 256), BF16),
        compiler_params=_cparams(("parallel", "arbitrary")),
        name="compress_sample",
    )(page_table, *([cache_t] * pages_per_step), *cw)


def _rep_rows(x, n):
    return jnp.concatenate([x] * n, axis=0)


def _qbd(nq, qb):
    zero = jnp.zeros((NSA_HPG * qb, NSA_HD), F32)
    blocks = []
    for g in range(NSA_GROUPS):
        parts = [nq[:, (g * NSA_HPG + h) * NSA_HD:(g * NSA_HPG + h + 1) * NSA_HD] for h in range(NSA_HPG)]
        qs = jnp.concatenate(parts, axis=0) * (NSA_HD ** -0.5 * LOG2E)
        blocks.append(jnp.concatenate([qs, zero] if g == 0 else [zero, qs], axis=1))
    return jnp.concatenate(blocks, axis=0).astype(BF16)


def _compressed_branch(qbd, kc, vc, pos_c, qb):
    nc = kc.shape[0]
    cmp_end = lax.broadcasted_iota(jnp.int32, (qb, nc), 1) * CMP_STRIDE + (CMP_LEN - 1)
    valid = jnp.where(cmp_end <= pos_c, 1.0, 0.0)
    s = _dot_nt(qbd, kc) + _rep_rows(jnp.where(valid > 0.5, 0.0, NEG_BIG), NSA_HEADS)
    e = jnp.exp2(s - jnp.maximum(jnp.max(s, axis=-1, keepdims=True), MASKED_MAX))
    p = e * (1.0 / jnp.maximum(jnp.sum(e, axis=-1, keepdims=True), 1e-30))
    return _dot(p.astype(BF16), vc), p


def _head_sum(p, g, qb):
    r0 = g * NSA_HPG * qb
    out = p[r0:r0 + qb]
    for h in range(1, NSA_HPG):
        out = out + p[r0 + h * qb:r0 + (h + 1) * qb]
    return out


def _select_blocks(imp_t, pos_r, n_pick, n_blocks):
    shape = imp_t.shape
    s_io = lax.broadcasted_iota(jnp.int32, shape, 0)
    cur = jnp.right_shift(pos_r, 6)
    valid = (s_io <= cur) & (s_io < n_blocks)
    forced = (s_io == 0) | (s_io == cur) | (s_io == cur - 1)
    bits = pltpu.bitcast(imp_t, jnp.int32)
    packed = pltpu.bitcast((bits & jnp.int32(-N_SLC_PAD)) | (N_SLC_PAD - 1 - s_io), F32)
    key = jnp.where(valid & jnp.logical_not(forced), packed, -1.0)
    sel = jnp.where(valid & forced, 1.0, 0.0)
    for _ in range(n_pick):
        m = jnp.max(key, axis=0, keepdims=True)
        pick = (key == m) & (m >= 0.0)
        sel = jnp.where(pick, 1.0, sel)
        key = jnp.where(pick, -1.0, key)
    return sel


def _unselected(p_sum, agg_t, pos_r, n_pick):
    hi, lo = _split(p_sum)
    imp_t = _dot_nt(agg_t, hi) + _dot_nt(agg_t, lo)
    n_blocks = p_sum.shape[1] * CMP_STRIDE // SLC_LEN
    sel_t = _select_blocks(imp_t, pos_r, n_pick, n_blocks)
    return (1.0 - sel_t.T).astype(BF16)


def _online_update(s, v, m, l, acc, v_transposed=False):
    m_new = jnp.maximum(m, jnp.max(s, axis=-1, keepdims=True))
    alpha = jnp.exp2(m - m_new)
    p = jnp.exp2(s - m_new)
    l_new = alpha * l + jnp.sum(p, axis=-1, keepdims=True)
    pv = _dot_nt(p.astype(BF16), v) if v_transposed else _dot(p.astype(BF16), v)
    return m_new, l_new, alpha * acc + pv


def _softmax_rows(s):
    e = jnp.exp2(s - jnp.max(s, axis=-1, keepdims=True))
    return e * (1.0 / jnp.sum(e, axis=-1, keepdims=True))


def _combine_and_store(o_ref, gsig, o_c, o_s, o_w, nn, qb):
    rows = NSA_HPG * qb
    for g in range(NSA_GROUPS):
        def gcol(j):
            cols = [gsig[:, SM_GATE_OFF + (g * NSA_HPG + h) * 3 + j:SM_GATE_OFF + (g * NSA_HPG + h) * 3 + j + 1]
                    for h in range(NSA_HPG)]
            return jnp.concatenate(cols, axis=0)

        def blk(x):
            return x[g * rows:(g + 1) * rows, g * NSA_HD:(g + 1) * NSA_HD]

        o = gcol(0) * blk(o_c) + gcol(1) * blk(o_s) + gcol(2) * blk(o_w)
        o = o * lax.rsqrt(jnp.mean(o * o, axis=-1, keepdims=True) + RMS_EPS) * nn
        for h in range(NSA_HPG):
            c0 = (g * NSA_HPG + h) * NSA_HD
            o_ref[:, c0:c0 + NSA_HD] = o[h * qb:(h + 1) * qb]


def _rep_lanes(x, n):
    return jnp.concatenate([x] * n, axis=1)


def _softmax_cols(s):
    e = jnp.exp2(s - jnp.max(s, axis=0, keepdims=True))
    return e * (1.0 / jnp.sum(e, axis=0, keepdims=True))


def _nsa_prompt_body(nq_ref, sm_ref, kcv_ref, kvb_ref, ebt_ref, agg_ref, gb_ref, nn_ref, o_ref, *, qb):
    i = pl.program_id(1)
    s0 = i * qb
    pos_r = s0 + lax.broadcasted_iota(jnp.int32, (1, qb), 1)
    cols = NSA_HEADS * qb
    nq_t = (nq_ref[...] * (NSA_HD ** -0.5 * LOG2E)).T
    zero = jnp.zeros((NSA_HD, qb), F32)
    q_rows = []
    for g in range(NSA_GROUPS):
        blks = [nq_t[gh * NSA_HD:(gh + 1) * NSA_HD, :] if gh // NSA_HPG == g else zero for gh in range(NSA_HEADS)]
        q_rows.append(jnp.concatenate(blks, axis=1))
    qbd_t = jnp.concatenate(q_rows, axis=0).astype(BF16)

    nc = kcv_ref.shape[1]
    cmp_end = lax.broadcasted_iota(jnp.int32, (nc, qb), 0) * CMP_STRIDE + (CMP_LEN - 1)
    valid = jnp.where(cmp_end <= pos_r, 1.0, 0.0)
    s_c = _dot(kcv_ref[0, :, 0:LANES], qbd_t) + _rep_lanes(jnp.where(valid > 0.5, 0.0, NEG_BIG), NSA_HEADS)
    e_c = jnp.exp2(s_c - jnp.maximum(jnp.max(s_c, axis=0, keepdims=True), MASKED_MAX))
    p_c = e_c * (1.0 / jnp.maximum(jnp.sum(e_c, axis=0, keepdims=True), 1e-30))
    o_c = _dot_tn(kcv_ref[0, :, LANES:2 * LANES], p_c.astype(BF16))

    unsel = []
    for g in range(NSA_GROUPS):
        p_sum = p_c[:, g * NSA_HPG * qb:(g * NSA_HPG + 1) * qb]
        for h in range(1, NSA_HPG):
            p_sum = p_sum + p_c[:, (g * NSA_HPG + h) * qb:(g * NSA_HPG + h + 1) * qb]
        hi, lo = _split(p_sum)
        imp_t = _dot(agg_ref[...], hi) + _dot(agg_ref[...], lo)
        sel_t = _select_blocks(imp_t, pos_r, SLC_TOP - 3, nc * CMP_STRIDE // SLC_LEN)
        unsel += [(1.0 - sel_t).astype(BF16)] * NSA_HPG
    lhs_t = jnp.concatenate([qbd_t, jnp.concatenate(unsel, axis=1)], axis=0)

    def scores(kt):
        krows = pl.ds(pl.multiple_of(kt * KV_TILE, KV_TILE), KV_TILE)
        rhs = jnp.concatenate([kvb_ref[0, krows, 0:LANES], ebt_ref[krows, :]], axis=1)
        return _dot(rhs, lhs_t)

    def causal_bias(kt):
        key_pos = kt * KV_TILE + lax.broadcasted_iota(jnp.int32, (KV_TILE, qb), 0)
        return _rep_lanes(jnp.where(key_pos <= pos_r, 0.0, NEG_BIG), NSA_HEADS)

    def update(kt, s, carry):
        krows = pl.ds(pl.multiple_of(kt * KV_TILE, KV_TILE), KV_TILE)
        m, l, acc = carry
        m_new = jnp.maximum(m, jnp.max(s, axis=0, keepdims=True))
        alpha = jnp.exp2(m - m_new)
        p = jnp.exp2(s - m_new)
        l_new = alpha * l + jnp.sum(p, axis=0, keepdims=True)
        acc_new = alpha * acc + _dot_tn(kvb_ref[0, krows, LANES:2 * LANES], p.astype(BF16))
        return m_new, l_new, acc_new

    def pair(kp, carry):
        s_a = scores(2 * kp)
        s_b = scores(2 * kp + 1)
        return update(2 * kp + 1, s_b, update(2 * kp, s_a, carry))

    n_tiles = (s0 + qb - 1) // KV_TILE + 1
    n_pairs = (n_tiles - 1) // 2
    last = kvb_ref.shape[1] // KV_TILE - 1
    init = (jnp.full((1, cols), NEG_BIG, F32), jnp.zeros((1, cols), F32), jnp.zeros((LANES, cols), F32))
    carry = lax.fori_loop(0, n_pairs, pair, init)
    kt_a = 2 * n_pairs
    kt_b = jnp.minimum(kt_a + 1, last)
    s_a = scores(kt_a) + causal_bias(kt_a)
    s_b = scores(kt_b) + causal_bias(kt_a + 1)
    _, l, acc = update(kt_b, s_b, update(kt_a, s_a, carry))
    o_s = acc * (1.0 / l)

    w_len = WINDOW + qb
    w_start = pl.multiple_of(jnp.maximum(s0 - WINDOW, 0), qb)
    wrows = pl.ds(w_start, w_len)
    pos_w = w_start + lax.broadcasted_iota(jnp.int32, (w_len, qb), 0)
    ok = (pos_w <= pos_r) & (pos_r - pos_w < WINDOW)
    s_w = _dot(kvb_ref[0, wrows, 2 * LANES:3 * LANES], qbd_t) + _rep_lanes(jnp.where(ok, 0.0, NEG_BIG), NSA_HEADS)
    o_w = _dot_tn(kvb_ref[0, wrows, 3 * LANES:4 * LANES], _softmax_cols(s_w).astype(BF16))

    gsig_t = _sigmoid(sm_ref[...] + gb_ref[...]).T
    for pair in range(NSA_HEADS // 2):
        blocks = []
        for hh in (2 * pair, 2 * pair + 1):
            g = hh // NSA_HPG

            def blk(x):
                return x[g * NSA_HD:(g + 1) * NSA_HD, hh * qb:(hh + 1) * qb]

            def gate(j):
                r = SM_GATE_OFF + hh * 3 + j
                return gsig_t[r:r + 1, :]

            o = gate(0) * blk(o_c) + gate(1) * blk(o_s) + gate(2) * blk(o_w)
            blocks.append(o * lax.rsqrt(jnp.mean(o * o, axis=0, keepdims=True) + RMS_EPS) * nn_ref[...])
        o_ref[:, pair * LANES:(pair + 1) * LANES] = jnp.concatenate(blocks, axis=0).T


def _nsa_prompt(nq, sm, kcv, kvb3, ebt, agg_t, gate_b, nsa_norm, nb, t_len):
    qb = 128
    nqb = t_len // qb
    nc = t_len // CMP_STRIDE
    row = lambda b, i: (b * nqb + i, 0)
    return pl.pallas_call(
        functools.partial(_nsa_prompt_body, qb=qb),
        grid=(nb, nqb),
        in_specs=[pl.BlockSpec((qb, 512), row), pl.BlockSpec((qb, LANES), row),
                  pl.BlockSpec((1, nc, 256), lambda b, i: (b, 0, 0)),
                  pl.BlockSpec((1, t_len, 512), lambda b, i: (b, 0, 0)),
                  pl.BlockSpec((t_len, N_SLC_PAD), lambda b, i: (0, 0)),
                  pl.BlockSpec((N_SLC_PAD, nc), lambda b, i: (0, 0)),
                  pl.BlockSpec((1, LANES), lambda b, i: (0, 0)),
                  pl.BlockSpec((NSA_HD, 1), lambda b, i: (0, 0))],
        out_specs=pl.BlockSpec((qb, 512), row),
        out_shape=jax.ShapeDtypeStruct((nb * t_len, 512), F32),
        compiler_params=_cparams(("parallel", "arbitrary")),
        name="nsa_prompt",
    )(nq, sm, kcv, kvb3, ebt, agg_t, gate_b, nsa_norm)


def _nsa_sample_body(*refs, n_new, past_len, pages_per_tile, w_buf):
    pages = refs[1:1 + pages_per_tile]
    (nq_ref, sm_ref, kcv_ref, new_ref, wnew_ref, win_ref, eb_ref, agg_ref, gb_ref, nn_ref, o_ref,
     lhs_ref, oc_ref, m_ref, l_ref, acc_ref) = refs[1 + pages_per_tile:]
    kt = pl.program_id(1)
    qb = n_new
    rows = NSA_HEADS * qb
    pos_c = past_len + lax.broadcasted_iota(jnp.int32, (qb, 1), 0)

    @pl.when(kt == 0)
    def _():
        qbd = _qbd(nq_ref[...], qb)
        o_c, p_c = _compressed_branch(qbd, kcv_ref[0, :, 0:LANES], kcv_ref[0, :, LANES:2 * LANES], pos_c, qb)
        oc_ref[...] = o_c
        pos_r = past_len + lax.broadcasted_iota(jnp.int32, (1, N_SLC_PAD), 1)
        unsel = []
        for g in range(NSA_GROUPS):
            p_sum = _head_sum(p_c, g, qb)
            p_pad = jnp.concatenate([p_sum, jnp.zeros((N_SLC_PAD - qb, p_sum.shape[1]), F32)], axis=0)
            unsel += [_unselected(p_pad, agg_ref[...], pos_r, SLC_TOP - 3)[0:qb]] * NSA_HPG
        lhs_ref[...] = jnp.concatenate([qbd, jnp.concatenate(unsel, axis=0)], axis=1)
        m_ref[...] = jnp.full((rows, 1), NEG_BIG, F32)
        l_ref[...] = jnp.zeros((rows, 1), F32)
        acc_ref[...] = jnp.zeros((rows, LANES), F32)

    k_t = jnp.concatenate([p[0, 0:LANES, :] for p in pages], axis=1).astype(BF16)
    v_t = jnp.concatenate([p[0, LANES:2 * LANES, :] for p in pages], axis=1).astype(BF16)
    s = _dot(lhs_ref[...], jnp.concatenate([k_t, eb_ref[0]], axis=0))
    m, l, acc = _online_update(s, v_t, m_ref[...], l_ref[...], acc_ref[...], v_transposed=True)
    m_ref[...] = m
    l_ref[...] = l
    acc_ref[...] = acc

    @pl.when(kt == pl.num_programs(1) - 1)
    def _():
        qbd = lhs_ref[:, 0:LANES]
        pad = jnp.zeros((LANES - n_new, LANES), F32)
        q_idx = lax.broadcasted_iota(jnp.int32, (qb, LANES), 0)
        lane = lax.broadcasted_iota(jnp.int32, (qb, LANES), 1)
        new_ok = _rep_rows(jnp.where((lane < n_new) & (lane <= q_idx), 0.0, NEG_BIG), NSA_HEADS)
        new = new_ref[...]
        k_new = jnp.concatenate([new[:, 2 * LANES:3 * LANES], pad], axis=0).astype(BF16)
        v_new = jnp.concatenate([new[:, 3 * LANES:4 * LANES], pad], axis=0).astype(BF16)
        _, l_f, acc_f = _online_update(_dot_nt(qbd, k_new) + new_ok, v_new, m, l, acc)
        o_s = acc_f * (1.0 / l_f)
        pos_w = (past_len - w_buf) + lax.broadcasted_iota(jnp.int32, (1, w_buf), 1)
        ok = (pos_w <= pos_c) & (pos_c - pos_w < WINDOW) & (pos_w >= 0)
        s_w = _dot(qbd, win_ref[0, 0:LANES, :].astype(BF16)) + _rep_rows(jnp.where(ok, 0.0, NEG_BIG), NSA_HEADS)
        wnew = wnew_ref[...]
        kw_new = jnp.concatenate([wnew[:, 0:LANES], pad], axis=0).astype(BF16)
        vw_new = jnp.concatenate([wnew[:, LANES:2 * LANES], pad], axis=0).astype(BF16)
        s_w2 = _dot_nt(qbd, kw_new) + new_ok
        p_w = _softmax_rows(jnp.concatenate([s_w, s_w2], axis=1)).astype(BF16)
        o_w = _dot_nt(p_w[:, 0:w_buf], win_ref[0, LANES:2 * LANES, :].astype(BF16)) + _dot(p_w[:, w_buf:], vw_new)
        gsig = _sigmoid(sm_ref[...] + gb_ref[...])
        _combine_and_store(o_ref, gsig, oc_ref[...], o_s, o_w, nn_ref[...], qb)


def _nsa_sample(nq, sm, kcv, rows_new, win_new, win_t, cache_t, page_table, eb, agg_t, gate_b, nsa_norm,
                nb, n_new, n_pages):
    past_len = n_pages * PAGE_SIZE
    pages_per_tile = min(n_pages, 16)
    n_tiles = n_pages // pages_per_tile
    nc = past_len // CMP_STRIDE
    w_buf = win_t.shape[2]
    rows = NSA_HEADS * n_new

    def page_spec(k):
        return pl.BlockSpec((1, 2 * LANES, PAGE_SIZE), lambda b, t, pt: (pt[b, t * pages_per_tile + k], 1, 0))

    row = lambda b, t, pt: (b, 0)
    const2 = lambda b, t, pt: (0, 0)
    grid_spec = pltpu.PrefetchScalarGridSpec(
        num_scalar_prefetch=1,
        grid=(nb, n_tiles),
        in_specs=[page_spec(k) for k in range(pages_per_tile)] + [
            pl.BlockSpec((n_new, 512), row), pl.BlockSpec((n_new, LANES), row),
            pl.BlockSpec((1, nc, 256), lambda b, t, pt: (b, 0, 0)),
            pl.BlockSpec((n_new, 512), row), pl.BlockSpec((n_new, 256), row),
            pl.BlockSpec((1, 256, w_buf), lambda b, t, pt: (b, 0, 0)),
            pl.BlockSpec((1, N_SLC_PAD, pages_per_tile * PAGE_SIZE), lambda b, t, pt: (t, 0, 0)),
            pl.BlockSpec((N_SLC_PAD, nc), const2),
            pl.BlockSpec((1, LANES), const2), pl.BlockSpec((1, NSA_HD), const2)],
        out_specs=pl.BlockSpec((n_new, 512), row),
        scratch_shapes=[pltpu.VMEM((rows, 2 * LANES), BF16), pltpu.VMEM((rows, LANES), F32),
                        pltpu.VMEM((rows, 1), F32), pltpu.VMEM((rows, 1), F32), pltpu.VMEM((rows, LANES), F32)],
    )
    return pl.pallas_call(
        functools.partial(_nsa_sample_body, n_new=n_new, past_len=past_len, pages_per_tile=pages_per_tile,
                          w_buf=w_buf),
        grid_spec=grid_spec,
        out_shape=jax.ShapeDtypeStruct((nb * n_new, 512), F32),
        compiler_params=_cparams(("parallel", "arbitrary")),
        name="nsa_sample",
    )(page_table, *([cache_t] * pages_per_tile), nq, sm, kcv, rows_new, win_new, win_t, eb, agg_t, gate_b, nsa_norm)


def _route(logits):
    lane = lax.broadcasted_iota(jnp.int32, logits.shape, 1)
    lane_f = lane.astype(F32)
    big = 1e6
    is_g = (lane >= N_EXPERTS) & (lane < N_EXPERTS + N_EXPERT_GROUPS)
    gl = jnp.where(is_g, logits, NEG_BIG)
    gmax = jnp.max(gl, axis=-1, keepdims=True)
    gtop = jnp.min(jnp.where(gl == gmax, lane_f, big), axis=-1, keepdims=True) - N_EXPERTS
    p_group = 1.0 / jnp.sum(jnp.exp(gl - gmax), axis=-1, keepdims=True)
    in_grp = (lane < N_EXPERTS) & (jnp.right_shift(lane, 3).astype(F32) == gtop)
    el = jnp.where(in_grp, logits, NEG_BIG)
    m1 = jnp.max(el, axis=-1, keepdims=True)
    i1 = jnp.min(jnp.where(el == m1, lane_f, big), axis=-1, keepdims=True)
    el2 = jnp.where(lane_f == i1, NEG_BIG, el)
    m2 = jnp.max(el2, axis=-1, keepdims=True)
    i2 = jnp.min(jnp.where(el2 == m2, lane_f, big), axis=-1, keepdims=True)
    r = jnp.exp(m2 - m1)
    t1 = 1.0 / (1.0 + r)
    t2 = r * t1
    return jnp.where(lane_f == i1, p_group * t1, 0.0) + jnp.where(lane_f == i2, p_group * t2, 0.0)


def _mix_body(og_ref, on_ref, x_ref, wo_ref, g_ref, b_ref, rwh_ref, rwl_ref, rb_ref, x1_ref, cmb_ref):
    mix = _dot(og_ref[...].astype(BF16), wo_ref[0:512, :]) + _dot(on_ref[...].astype(BF16), wo_ref[512:1024, :])
    x1 = _layer_norm(DEEPNORM_ALPHA * x_ref[...] + mix, g_ref[...], b_ref[...])
    x1_ref[...] = x1
    logits = _dot3(x1, rwh_ref[...], rwl_ref[...]) + rb_ref[...]
    cmb_ref[...] = _route(logits)


def _mix(o_gla, o_nsa, x2d, wo, g, b, rw_hi, rw_lo, rb):
    n = x2d.shape[0]
    tm = min(n, 512)
    row = lambda i: (i, 0)
    const = lambda i: (0, 0)
    return pl.pallas_call(
        _mix_body,
        grid=(n // tm,),
        in_specs=[pl.BlockSpec((tm, 512), row), pl.BlockSpec((tm, 512), row), pl.BlockSpec((tm, D_MODEL), row),
                  pl.BlockSpec((D_MODEL, D_MODEL), const), pl.BlockSpec((1, D_MODEL), const),
                  pl.BlockSpec((1, D_MODEL), const), pl.BlockSpec((D_MODEL, LANES), const),
                  pl.BlockSpec((D_MODEL, LANES), const), pl.BlockSpec((1, LANES), const)],
        out_specs=[pl.BlockSpec((tm, D_MODEL), row), pl.BlockSpec((tm, LANES), row)],
        out_shape=[jax.ShapeDtypeStruct((n, D_MODEL), F32), jax.ShapeDtypeStruct((n, LANES), F32)],
        compiler_params=_cparams(("parallel",)),
        name="mix",
    )(o_gla, o_nsa, x2d, wo, g, b, rw_hi, rw_lo, rb)


def _moe_body(x1_ref, cmb_ref, tri_ref, wg_ref, wu_ref, wd_ref, g_ref, b_ref, o_ref,
              acc_ref, xb_ref, key_ref, keyt_ref, cmbt_ref, *, chunk):
    j = pl.program_id(1)
    tm = x1_ref.shape[0]

    @pl.when(j == 0)
    def _():
        acc_ref[...] = jnp.zeros_like(acc_ref)
        xb_ref[...] = x1_ref[...].astype(BF16)
        cmb = cmb_ref[...]
        routed = cmb != 0.0
        rank = _dot(tri_ref[...], jnp.where(routed, 1.0, 0.0).astype(BF16))
        key = jnp.where(routed, rank, -1.0)
        key_ref[...] = key
        keyt_ref[...] = key.T
        cmbt_ref[...] = cmb.T

    lane = lax.broadcasted_iota(jnp.int32, (tm, LANES), 1)
    r_io = lax.broadcasted_iota(jnp.int32, (chunk, tm), 0).astype(F32)
    c_io = lax.broadcasted_iota(jnp.int32, (tm, chunk), 1).astype(F32)
    key_rows, w_rows, key_cols, counts = [], [], [], []
    for k in range(MOE_PAIR):
        e = MOE_PAIR * j + k
        key_rows.append(keyt_ref[pl.ds(e, 1), :])
        w_rows.append(cmbt_ref[pl.ds(e, 1), :])
        key_cols.append(jnp.sum(jnp.where(lane == e, key_ref[...], 0.0), axis=-1, keepdims=True))
        counts.append(jnp.sum(jnp.where(key_rows[k] >= 0.0, 1.0, 0.0)).astype(jnp.int32))

    def one_chunk(c, carry):
        base = (c * chunk).astype(F32)
        sels = [jnp.where(key_rows[k] - base == r_io, 1.0, 0.0) for k in range(MOE_PAIR)]
        xe = _dot(jnp.concatenate(sels, axis=0).astype(BF16), xb_ref[...]).astype(BF16)
        ys = []
        for k in range(MOE_PAIR):
            xk = xe[k * chunk:(k + 1) * chunk]
            hdn = _silu(_dot(xk, wg_ref[k])) * _dot(xk, wu_ref[k])
            wc = jnp.sum(sels[k] * w_rows[k], axis=-1, keepdims=True)
            ys.append((wc * _dot(hdn.astype(BF16), wd_ref[k])).astype(BF16))
        sel_t = jnp.concatenate([jnp.where(key_cols[k] - base == c_io, 1.0, 0.0) for k in range(MOE_PAIR)],
                                axis=1).astype(BF16)
        acc_ref[...] += _dot(sel_t, jnp.concatenate(ys, axis=0))
        return carry

    n_chunks = (functools.reduce(jnp.maximum, counts) + chunk - 1) // chunk
    lax.fori_loop(0, n_chunks, one_chunk, 0)

    @pl.when(j == pl.num_programs(1) - 1)
    def _():
        o_ref[...] = _layer_norm(DEEPNORM_ALPHA * x1_ref[...] + acc_ref[...], g_ref[...], b_ref[...])


def _moe(x1, cmb, wg, wu, wd, g, b):
    n = x1.shape[0]
    tm = min(n, MOE_TILE)
    chunk = min(tm, MOE_CHUNK)
    tri = jnp.where(jnp.arange(tm)[:, None] > jnp.arange(tm)[None, :], 1.0, 0.0).astype(BF16)
    row = lambda i, j: (i, 0)
    const = lambda i, j: (0, 0)
    return pl.pallas_call(
        functools.partial(_moe_body, chunk=chunk),
        grid=(n // tm, N_EXPERTS // MOE_PAIR),
        in_specs=[pl.BlockSpec((tm, D_MODEL), row), pl.BlockSpec((tm, LANES), row),
                  pl.BlockSpec((tm, tm), const),
                  pl.BlockSpec((MOE_PAIR, D_MODEL, D_EXPERT), lambda i, j: (j, 0, 0)),
                  pl.BlockSpec((MOE_PAIR, D_MODEL, D_EXPERT), lambda i, j: (j, 0, 0)),
                  pl.BlockSpec((MOE_PAIR, D_EXPERT, D_MODEL), lambda i, j: (j, 0, 0)),
                  pl.BlockSpec((1, D_MODEL), const), pl.BlockSpec((1, D_MODEL), const)],
        out_specs=pl.BlockSpec((tm, D_MODEL), row),
        out_shape=jax.ShapeDtypeStruct((n, D_MODEL), F32),
        scratch_shapes=[pltpu.VMEM((tm, D_MODEL), F32), pltpu.VMEM((tm, D_MODEL), BF16),
                        pltpu.VMEM((tm, LANES), F32), pltpu.VMEM((LANES, tm), F32), pltpu.VMEM((LANES, tm), F32)],
        compiler_params=_cparams(("parallel", "arbitrary")),
        name="moe",
    )(x1, cmb, tri, wg, wu, wd, g, b)


def _prep_weights(w_in, gla_w_a, gla_b_a, nsa_gate_b, cmp_pe, cmp_w1, cmp_w2, gla_norm, nsa_norm, w_o,
                  ln1_g, ln1_b, router_g_w, router_g_b, router_e_w, router_e_b, w_gate, w_up, w_down,
                  ln2_g, ln2_b, n_cmp):
    o_glr, o_gog, o_nq, o_nkv, o_ngt = 1024, 1040, 1552, 2064, 2832
    pad = jnp.zeros((D_MODEL, LANES - GLA_RANK - 3 * NSA_HEADS), F32)
    w_proj = jnp.concatenate([w_in[:, 0:512], w_in[:, 512:1024], w_in[:, o_gog:o_gog + 512],
                              w_in[:, o_nq:o_nq + 512], w_in[:, o_nkv:o_nkv + 768],
                              w_in[:, o_glr:o_glr + GLA_RANK], w_in[:, o_ngt:o_ngt + 24], pad], axis=1).astype(BF16)
    wa = jnp.zeros((LANES, GLA_HEADS * GLA_DK), F32).at[0:GLA_RANK].set(gla_w_a)
    wa_hi, wa_lo = _split(wa)
    ba = gla_b_a.reshape(1, -1)
    gate_b = jnp.zeros((1, LANES), F32).at[0, SM_GATE_OFF:SM_GATE_OFF + 24].set(nsa_gate_b)
    eye = jnp.eye(NSA_GROUPS, dtype=F32)
    w1r = cmp_w1.reshape(2, CMP_LEN, NSA_HD, CMP_HIDDEN)

    def split_w1(w):
        wa_ = jnp.einsum('ldj,gh->lgdhj', w[:CMP_STRIDE], eye).reshape(CMP_STRIDE * LANES, 2 * CMP_HIDDEN)
        wb_ = jnp.einsum('ldj,gh->lgdhj', w[CMP_STRIDE:], eye).reshape(CMP_STRIDE * LANES, 2 * CMP_HIDDEN)
        return jnp.concatenate([wa_, wb_], axis=1).astype(BF16)

    wk, wv = split_w1(w1r[0]), split_w1(w1r[1])
    pe = cmp_pe.reshape(2, CMP_LEN * NSA_HD)
    w1_hi, w1_lo = _split(cmp_w1)
    w2bd = jnp.einsum('sjd,gh->sgjhd', cmp_w2, eye).reshape(2, 2 * CMP_HIDDEN, LANES).astype(BF16)
    cw = (wk, wv, pe, w1_hi, w1_lo, w2bd)
    starts = np.arange(n_cmp) * CMP_STRIDE
    sel_starts = np.arange(N_SLC_PAD) * SLC_LEN
    agg_t = ((starts[None, :] < sel_starts[:, None] + SLC_LEN)
             & (starts[None, :] + CMP_LEN > sel_starts[:, None])
             & (np.arange(n_cmp)[None, :] < n_cmp - 1))
    agg_t = jnp.asarray(agg_t, BF16)
    rw = jnp.zeros((D_MODEL, LANES), F32)
    rw = rw.at[:, 0:N_EXPERTS].set(router_e_w.transpose(1, 0, 2).reshape(D_MODEL, N_EXPERTS))
    rw = rw.at[:, N_EXPERTS:N_EXPERTS + N_EXPERT_GROUPS].set(router_g_w)
    rw_hi, rw_lo = _split(rw)
    rb = jnp.zeros((1, LANES), F32).at[0, 0:N_EXPERTS].set(router_e_b.reshape(-1))
    rb = rb.at[0, N_EXPERTS:N_EXPERTS + N_EXPERT_GROUPS].set(router_g_b)
    return dict(w_proj=w_proj, wa_hi=wa_hi, wa_lo=wa_lo, ba=ba, gate_b=gate_b, cw=cw, agg_t=agg_t,
                gn=gla_norm.reshape(1, -1), nn=nsa_norm.reshape(1, -1), wo=w_o.astype(BF16),
                ln1_g=ln1_g.reshape(1, -1), ln1_b=ln1_b.reshape(1, -1), rw_hi=rw_hi, rw_lo=rw_lo, rb=rb,
                wg=w_gate.astype(BF16), wu=w_up.astype(BF16), wd=w_down.astype(BF16),
                ln2_g=ln2_g.reshape(1, -1), ln2_b=ln2_b.reshape(1, -1))


def _tail(w, o_gla, o_nsa, x2d):
    x1, cmb = _mix(o_gla, o_nsa, x2d, w['wo'], w['ln1_g'], w['ln1_b'], w['rw_hi'], w['rw_lo'], w['rb'])
    return _moe(x1, cmb, w['wg'], w['wu'], w['wd'], w['ln2_g'], w['ln2_b'])


def kernel(x_prompt, x_sample, cache_nsa, state_win, state_gla, page_table, w_in, gla_w_a, gla_b_a, nsa_gate_b,
           cmp_pe, cmp_w1, cmp_w2, gla_norm, nsa_norm, w_o, ln1_g, ln1_b, router_g_w, router_g_b, router_e_w,
           router_e_b, w_gate, w_up, w_down, ln2_g, ln2_b):
    assert w_in.shape[0] == 1, "single layer"
    bp, tp, _ = x_prompt.shape
    bs, ts, _ = x_sample.shape
    n_pages = page_table.shape[1]
    past_len = n_pages * PAGE_SIZE
    assert tp == past_len, "prompt and past share the compressed-block count"
    w = _prep_weights(w_in[0], gla_w_a[0], gla_b_a[0], nsa_gate_b[0], cmp_pe[0], cmp_w1[0], cmp_w2[0],
                      gla_norm[0], nsa_norm[0], w_o[0], ln1_g[0], ln1_b[0], router_g_w[0], router_g_b[0],
                      router_e_w[0], router_e_b[0], w_gate[0], w_up[0], w_down[0], ln2_g[0], ln2_b[0],
                      n_cmp=tp // CMP_STRIDE)

    xp2 = x_prompt.reshape(bp * tp, D_MODEL)
    qk, v, og, nq, rows4, win, kvb, sm, rows_t = _proj(xp2, w['w_proj'], seq_len=tp)
    o_gla, gla_p = _gla(qk, v, og, sm, w['wa_hi'], w['wa_lo'], w['ba'], w['gn'], None, bp, tp)
    kcv = _compress_prompt(rows4, w['cw'], bp, tp)
    blk_of_key = jnp.arange(tp, dtype=jnp.int32) // SLC_LEN
    ebt = jnp.where(blk_of_key[:, None] == jnp.arange(N_SLC_PAD, dtype=jnp.int32)[None, :], NEG_BIG, 0.0).astype(BF16)
    o_nsa = _nsa_prompt(nq, sm, kcv, kvb.reshape(bp, tp, 512), ebt, w['agg_t'], w['gate_b'], w['nn'].reshape(-1, 1),
                        bp, tp)
    y_p = _tail(w, o_gla, o_nsa, xp2).reshape(bp, tp, D_MODEL)
    w_rows = min(WINDOW, tp)
    rows_p = rows_t.reshape(bp, 4, NSA_GROUPS, NSA_HD, tp).transpose(0, 4, 1, 2, 3)[None]
    win_p = win.reshape(bp, tp, 2, NSA_GROUPS, NSA_HD)[None, :, tp - w_rows:]

    xs2 = x_sample.reshape(bs * ts, D_MODEL)
    qk, v, og, nq, rows4s, win_new, _, sm = _proj(xs2, w['w_proj'])
    o_gla, gla_s = _gla(qk, v, og, sm, w['wa_hi'], w['wa_lo'], w['ba'], w['gn'], state_gla[0], bs, ts)
    cache_t = cache_nsa[0].transpose(0, 2, 3, 4, 1).reshape(-1, 4 * NSA_GROUPS * NSA_HD, PAGE_SIZE)
    w_buf = state_win.shape[2]
    win_t = state_win[0].transpose(0, 2, 3, 4, 1).reshape(bs, 2 * NSA_GROUPS * NSA_HD, w_buf)
    kcv = _compress_sample(cache_t, page_table, w['cw'], bs, n_pages)
    tile_keys = min(n_pages, 16) * PAGE_SIZE
    eb = jnp.where(blk_of_key.reshape(-1, 1, tile_keys) == jnp.arange(N_SLC_PAD, dtype=jnp.int32)[None, :, None],
                   NEG_BIG, 0.0).astype(BF16)
    o_nsa = _nsa_sample(nq, sm, kcv, rows4s, win_new, win_t, cache_t, page_table, eb, w['agg_t'], w['gate_b'],
                        w['nn'], bs, ts, n_pages)
    y_s = _tail(w, o_gla, o_nsa, xs2).reshape(bs, ts, D_MODEL)
    rows_s = rows4s.reshape(1, bs, ts, 4, NSA_GROUPS, NSA_HD)
    win_s = jnp.concatenate([state_win[0][:, ts:], win_new.reshape(bs, ts, 2, NSA_GROUPS, NSA_HD)], axis=1)[None]

    return (y_p, y_s, rows_p, rows_s, win_p, win_s, gla_p[None], gla_s[None])
```

```python
import functools

import numpy as np
import jax
import jax.numpy as jnp
from jax import lax
from jax.experimental import pallas as pl
from jax.experimental.pallas import tpu as pltpu

F32 = jnp.float32
BF16 = jnp.bfloat16

D_MODEL = 1024
GLA_HEADS = 4
GLA_DK = 64
GLA_DV = 128
GLA_RANK = 16
GLA_TAU = 16.0
GLA_CHUNK = 64
NSA_HEADS = 8
NSA_GROUPS = 2
NSA_HPG = 4
NSA_HD = 64
CMP_LEN = 32
CMP_STRIDE = 16
CMP_HIDDEN = 128
SLC_LEN = 64
SLC_TOP = 16
WINDOW = 512
PAGE_SIZE = 128
N_EXPERT_GROUPS = 4
EXPERTS_PER_GROUP = 8
N_EXPERTS = 32
D_EXPERT = 512
DEPTH = 1
DEEPNORM_ALPHA = (2.0 * DEPTH) ** 0.25
LN_EPS = 1e-5
RMS_EPS = 1e-6
NEG_BIG = -1e30
MASKED_MAX = -1e29
LOG2E = 1.4426950408889634
FORCE_SCORE = 1e9

LANES = 128
KV_TILE = 512
BLK_PER_TILE = KV_TILE // SLC_LEN
N_SLC_PAD = 128
VMEM_LIMIT = 56 * 1024 * 1024
MOE_TILE = 1024
MOE_CHUNK = 128
MOE_PAIR = 2

SEG_QK, SEG_V, SEG_OG, SEG_NQ, SEG_ROWS, SEG_WIN, SEG_SM = 0, 512, 1024, 1536, 2048, 2560, 2816
D_PROJ = 2944
SM_GATE_OFF = GLA_RANK


def _cparams(sem):
    return pltpu.CompilerParams(dimension_semantics=sem, vmem_limit_bytes=VMEM_LIMIT)


def _dot(a, b):
    return jnp.dot(a, b, preferred_element_type=F32)


def _dot_nt(a, b):
    return lax.dot_general(a, b, (((1,), (1,)), ((), ())), preferred_element_type=F32)


def _dot_tn(a, b):
    return lax.dot_general(a, b, (((0,), (0,)), ((), ())), preferred_element_type=F32)


def _split(x):
    hi = x.astype(BF16)
    lo = (x - hi.astype(F32)).astype(BF16)
    return hi, lo


def _dot3(a, b_hi, b_lo):
    a_hi, a_lo = _split(a)
    return _dot(a_hi, b_hi) + (_dot(a_hi, b_lo) + _dot(a_lo, b_hi))


def _sigmoid(x):
    return 1.0 / (1.0 + jnp.exp(-x))


def _silu(x):
    return x * _sigmoid(x)


def _layer_norm(y, g, b):
    mu = jnp.mean(y, axis=-1, keepdims=True)
    d = y - mu
    var = jnp.mean(d * d, axis=-1, keepdims=True)
    return d * lax.rsqrt(var + LN_EPS) * g + b


def _proj_body(x_ref, w_ref, qk_ref, v_ref, og_ref, nq_ref, rows_ref, win_ref, kvb_ref, sm_ref, rows_t_ref=None):
    x = x_ref[...].astype(BF16)

    def seg(off, width):
        return _dot(x, w_ref[:, off:off + width])

    qk_ref[...] = seg(SEG_QK, 512)
    v_ref[...] = seg(SEG_V, 512)
    og_ref[...] = seg(SEG_OG, 512)
    nq_ref[...] = seg(SEG_NQ, 512)
    rows = seg(SEG_ROWS, 512)
    rows_ref[...] = rows
    if rows_t_ref is not None:
        rows_t_ref[0] = rows.T
    win = seg(SEG_WIN, 256)
    win_ref[...] = win
    kvb_ref[:, 0:256] = rows[:, 256:512].astype(BF16)
    kvb_ref[:, 256:512] = win.astype(BF16)
    sm_ref[...] = seg(SEG_SM, 128)


def _proj(x2d, w_proj, seq_len=None):
    n = x2d.shape[0]
    tm = min(n, 512)
    widths = (512, 512, 512, 512, 512, 256, 512, 128)
    dtypes = (F32, F32, F32, F32, F32, F32, BF16, F32)
    out_specs = [pl.BlockSpec((tm, w), lambda i: (i, 0)) for w in widths]
    out_shape = [jax.ShapeDtypeStruct((n, w), d) for w, d in zip(widths, dtypes)]
    if seq_len is not None:
        nt = seq_len // tm
        out_specs.append(pl.BlockSpec((1, 512, tm), lambda i: (i // nt, 0, i % nt)))
        out_shape.append(jax.ShapeDtypeStruct((n // seq_len, 512, seq_len), F32))
    return pl.pallas_call(
        _proj_body,
        grid=(n // tm,),
        in_specs=[pl.BlockSpec((tm, D_MODEL), lambda i: (i, 0)),
                  pl.BlockSpec((D_MODEL, D_PROJ), lambda i: (0, 0))],
        out_specs=out_specs,
        out_shape=out_shape,
        compiler_params=_cparams(("parallel",)),
        name="proj",
    )(x2d, w_proj)


def _gla_body(*refs, t_real, n_chunks, has_s0):
    if has_s0:
        qk_ref, v_ref, og_ref, sm_ref, wah_ref, wal_ref, ba_ref, gn_ref, s0_ref, o_ref, sf_ref, st_ref = refs
    else:
        qk_ref, v_ref, og_ref, sm_ref, wah_ref, wal_ref, ba_ref, gn_ref, o_ref, sf_ref, st_ref = refs
        s0_ref = None
    c_len = GLA_CHUNK
    t = pl.program_id(1)

    @pl.when(t == 0)
    def _():
        for h in range(GLA_HEADS):
            if has_s0:
                st_ref[h] = s0_ref[0, h].T
            else:
                st_ref[h] = jnp.zeros((GLA_DV, GLA_DK), F32)

    r_io = lax.broadcasted_iota(jnp.int32, (c_len, c_len), 0)
    c_io = lax.broadcasted_iota(jnp.int32, (c_len, c_len), 1)
    causal = r_io >= c_io
    padded = t_real < c_len
    t_rows = n_chunks * c_len

    def load(ref, rows):
        if padded:
            x = ref[...]
            return jnp.concatenate([x, jnp.zeros((c_len - t_real, x.shape[1]), x.dtype)], axis=0)
        return ref[rows, :]

    z = _dot3(load(sm_ref, pl.ds(0, t_rows)), wah_ref[...], wal_ref[...]) + ba_ref[...]
    la_all = (jnp.minimum(z, 0.0) - jnp.log1p(jnp.exp(-jnp.abs(z)))) / GLA_TAU
    if padded:
        row_id = lax.broadcasted_iota(jnp.int32, la_all.shape, 0)
        la_all = jnp.where(row_id < t_real, la_all, 0.0)
    rr = lax.broadcasted_iota(jnp.int32, (t_rows, t_rows), 0)
    cc = lax.broadcasted_iota(jnp.int32, (t_rows, t_rows), 1)
    tri = jnp.where((rr >= cc) & (jnp.right_shift(rr, 6) == jnp.right_shift(cc, 6)), 1.0, 0.0).astype(BF16)
    la_hi, la_lo = _split(la_all)
    cum_all = _dot(tri, la_hi) + _dot(tri, la_lo)

    for c in range(n_chunks):
        rows = pl.ds(c * c_len, c_len)
        qk = load(qk_ref, rows)
        v = load(v_ref, rows)
        cum = cum_all[c * c_len:(c + 1) * c_len]
        last = cum[c_len - 1:c_len, :]
        e_q = jnp.exp(cum)
        e_k = jnp.exp(-cum)
        e_kd = jnp.exp(last - cum)
        e_l = jnp.exp(last)
        for h in range(GLA_HEADS):
            sl = slice(h * GLA_DK, (h + 1) * GLA_DK)
            qh = qk[:, sl] * (GLA_DK ** -0.5)
            kh = qk[:, GLA_HEADS * GLA_DK + h * GLA_DK:GLA_HEADS * GLA_DK + (h + 1) * GLA_DK]
            vh = v[:, h * GLA_DV:(h + 1) * GLA_DV].astype(BF16)
            q_dec = (qh * e_q[:, sl]).astype(BF16)
            k_inv = (kh * e_k[:, sl]).astype(BF16)
            k_dec = (kh * e_kd[:, sl]).astype(BF16)
            att = jnp.where(causal, _dot_nt(q_dec, k_inv), 0.0).astype(BF16)
            s_t = st_ref[h]
            o = _dot(att, vh) + _dot_nt(q_dec, s_t.astype(BF16))
            st_ref[h] = s_t * e_l[:, sl] + _dot_tn(vh, k_dec)
            gate = _silu(load(og_ref, rows)[:, h * GLA_DV:(h + 1) * GLA_DV])
            on = o * lax.rsqrt(jnp.mean(o * o, axis=-1, keepdims=True) + RMS_EPS) * gn_ref[...]
            res = on * gate
            if padded:
                o_ref[:, h * GLA_DV:(h + 1) * GLA_DV] = res[:t_real]
            else:
                o_ref[rows, h * GLA_DV:(h + 1) * GLA_DV] = res

    @pl.when(t == pl.num_programs(1) - 1)
    def _():
        for h in range(GLA_HEADS):
            sf_ref[0, h] = st_ref[h].T


def _gla(qk, v, og, sm, wa_hi, wa_lo, ba, gn, s0, nb, t_len):
    n = nb * t_len
    if t_len >= GLA_CHUNK:
        tt = min(t_len, 512)
        t_real = GLA_CHUNK
        n_chunks = tt // GLA_CHUNK
    else:
        tt = t_len
        t_real = t_len
        n_chunks = 1
    nt = t_len // tt
    row = lambda b, t: (b * nt + t, 0)
    const = lambda b, t: (0, 0)
    in_specs = [pl.BlockSpec((tt, 512), row), pl.BlockSpec((tt, 512), row), pl.BlockSpec((tt, 512), row),
                pl.BlockSpec((tt, LANES), row),
                pl.BlockSpec((LANES, 256), const), pl.BlockSpec((LANES, 256), const),
                pl.BlockSpec((1, 256), const), pl.BlockSpec((1, GLA_DV), const)]
    args = [qk, v, og, sm, wa_hi, wa_lo, ba, gn]
    if s0 is not None:
        in_specs.append(pl.BlockSpec((1, GLA_HEADS, GLA_DK, GLA_DV), lambda b, t: (b, 0, 0, 0)))
        args.append(s0)
    return pl.pallas_call(
        functools.partial(_gla_body, t_real=t_real, n_chunks=n_chunks, has_s0=s0 is not None),
        grid=(nb, nt),
        in_specs=in_specs,
        out_specs=[pl.BlockSpec((tt, 512), row),
                   pl.BlockSpec((1, GLA_HEADS, GLA_DK, GLA_DV), lambda b, t: (b, 0, 0, 0))],
        out_shape=[jax.ShapeDtypeStruct((n, 512), F32),
                   jax.ShapeDtypeStruct((nb, GLA_HEADS, GLA_DK, GLA_DV), F32)],
        scratch_shapes=[pltpu.VMEM((GLA_HEADS, GLA_DV, GLA_DK), F32)],
        compiler_params=_cparams(("parallel", "arbitrary")),
        name="gla",
    )(*args)


def _gelu_tanh(x):
    return 0.5 * x * (1.0 + jnp.tanh(0.7978845608028654 * (x + 0.044715 * x * x * x)))


def _compress_body(*refs, n_pages, rows_per_step, n_groups):
    if n_pages:
        pages = refs[1:1 + n_pages]
        wk_ref, wv_ref, pe_ref, w1h_ref, w1l_ref, w2_ref, out_ref, p_ref, rk_ref, rv_ref = refs[1 + n_pages:]
        for idx, page in enumerate(pages):
            rk_ref[idx * PAGE_SIZE:(idx + 1) * PAGE_SIZE, :] = page[0, 0:LANES, :].T
            rv_ref[idx * PAGE_SIZE:(idx + 1) * PAGE_SIZE, :] = page[0, LANES:2 * LANES, :].T
    else:
        rk_ref, rv_ref, wk_ref, wv_ref, pe_ref, w1h_ref, w1l_ref, w2_ref, out_ref, p_ref = refs
    j = pl.program_id(1)
    g_step = rows_per_step // CMP_STRIDE

    def gather(r):
        cols = [r[pl.ds(l, g_step, stride=CMP_STRIDE), :] for l in range(CMP_STRIDE)]
        return jnp.concatenate(cols, axis=1).astype(BF16)

    rows = pl.ds(pl.multiple_of(j * g_step, g_step), g_step)
    p_ref[rows, 0:512] = _dot(gather(rk_ref), wk_ref[...])
    p_ref[rows, 512:1024] = _dot(gather(rv_ref), wv_ref[...])

    @pl.when(j == pl.num_programs(1) - 1)
    def _():
        for s in range(2):
            p = p_ref[:, s * 512:(s + 1) * 512]
            pe = jnp.broadcast_to(pe_ref[s:s + 1, :], (8, CMP_LEN * NSA_HD))
            bias = _dot3(pe, w1h_ref[s], w1l_ref[s])[0:1, :]
            bias2 = jnp.concatenate([bias, bias], axis=1)
            h = p[:, 0:256] + pltpu.roll(p[:, 256:512], n_groups - 1, 0) + bias2
            out_ref[0, :, s * LANES:(s + 1) * LANES] = _dot(_gelu_tanh(h).astype(BF16), w2_ref[s]).astype(BF16)


def _compress_specs_tail():
    c2 = lambda b, j: (0, 0)
    c3 = lambda b, j: (0, 0, 0)
    return [pl.BlockSpec((16 * LANES, 512), c2), pl.BlockSpec((16 * LANES, 512), c2),
            pl.BlockSpec((2, CMP_LEN * NSA_HD), c2),
            pl.BlockSpec((2, CMP_LEN * NSA_HD, CMP_HIDDEN), c3),
            pl.BlockSpec((2, CMP_LEN * NSA_HD, CMP_HIDDEN), c3),
            pl.BlockSpec((2, 2 * CMP_HIDDEN, LANES), c3)]


def _compress_prompt(rows4, cw, nb, t_len):
    chunk = min(t_len, 2048)
    nj = t_len // chunk
    n_groups = t_len // CMP_STRIDE
    return pl.pallas_call(
        functools.partial(_compress_body, n_pages=0, rows_per_step=chunk, n_groups=n_groups),
        grid=(nb, nj),
        in_specs=[pl.BlockSpec((chunk, LANES), lambda b, j: (b * nj + j, 0)),
                  pl.BlockSpec((chunk, LANES), lambda b, j: (b * nj + j, 1))] + _compress_specs_tail(),
        out_specs=pl.BlockSpec((1, n_groups, 256), lambda b, j: (b, 0, 0)),
        out_shape=jax.ShapeDtypeStruct((nb, n_groups, 256), BF16),
        scratch_shapes=[pltpu.VMEM((n_groups, 1024), F32)],
        compiler_params=_cparams(("parallel", "arbitrary")),
        name="compress_prompt",
    )(rows4, rows4, *cw)


def _compress_sample(cache_t, page_table, cw, nb, n_pages):
    pages_per_step = min(n_pages, 16)
    nj = n_pages // pages_per_step
    n_groups = n_pages * PAGE_SIZE // CMP_STRIDE
    rows_per_step = pages_per_step * PAGE_SIZE

    def page_spec(k):
        return pl.BlockSpec((1, 2 * LANES, PAGE_SIZE), lambda b, j, pt: (pt[b, j * pages_per_step + k], 0, 0))

    tail = _compress_specs_tail()
    tail = [pl.BlockSpec(s.block_shape, (lambda f: (lambda b, j, pt: f(b, j)))(s.index_map)) for s in tail]
    grid_spec = pltpu.PrefetchScalarGridSpec(
        num_scalar_prefetch=1,
        grid=(nb, nj),
        in_specs=[page_spec(k) for k in range(pages_per_step)] + tail,
        out_specs=pl.BlockSpec((1, n_groups, 256), lambda b, j, pt: (b, 0, 0)),
        scratch_shapes=[pltpu.VMEM((n_groups, 1024), F32), pltpu.VMEM((rows_per_step, LANES), F32),
                        pltpu.VMEM((rows_per_step, LANES), F32)],
    )
    return pl.pallas_call(
        functools.partial(_compress_body, n_pages=pages_per_step, rows_per_step=rows_per_step, n_groups=n_groups),
        grid_spec=grid_spec,
        out_shape=jax.ShapeDtypeStruct((nb, n_groups, 256), BF16),
        compiler_params=_cparams(("parallel", "arbitrary")),
        name="compress_sample",
    )(page_table, *([cache_t] * pages_per_step), *cw)


def _rep_rows(x, n):
    return jnp.concatenate([x] * n, axis=0)


def _qbd(nq, qb):
    zero = jnp.zeros((NSA_HPG * qb, NSA_HD), F32)
    blocks = []
    for g in range(NSA_GROUPS):
        parts = [nq[:, (g * NSA_HPG + h) * NSA_HD:(g * NSA_HPG + h + 1) * NSA_HD] for h in range(NSA_HPG)]
        qs = jnp.concatenate(parts, axis=0) * (NSA_HD ** -0.5 * LOG2E)
        blocks.append(jnp.concatenate([qs, zero] if g == 0 else [zero, qs], axis=1))
    return jnp.concatenate(blocks, axis=0).astype(BF16)


def _compressed_branch(qbd, kc, vc, pos_c, qb):
    nc = kc.shape[0]
    cmp_end = lax.broadcasted_iota(jnp.int32, (qb, nc), 1) * CMP_STRIDE + (CMP_LEN - 1)
    valid = jnp.where(cmp_end <= pos_c, 1.0, 0.0)
    s = _dot_nt(qbd, kc) + _rep_rows(jnp.where(valid > 0.5, 0.0, NEG_BIG), NSA_HEADS)
    e = jnp.exp2(s - jnp.maximum(jnp.max(s, axis=-1, keepdims=True), MASKED_MAX))
    p = e * (1.0 / jnp.maximum(jnp.sum(e, axis=-1, keepdims=True), 1e-30))
    return _dot(p.astype(BF16), vc), p


def _head_sum(p, g, qb):
    r0 = g * NSA_HPG * qb
    out = p[r0:r0 + qb]
    for h in range(1, NSA_HPG):
        out = out + p[r0 + h * qb:r0 + (h + 1) * qb]
    return out


def _select_blocks(imp_t, pos_r, n_pick, n_blocks):
    shape = imp_t.shape
    s_io = lax.broadcasted_iota(jnp.int32, shape, 0)
    s_f = s_io.astype(F32)
    cur = jnp.right_shift(pos_r, 6)
    valid = (s_io <= cur) & (s_io < n_blocks)
    forced = (s_io == 0) | (s_io == cur) | (s_io == cur - 1)
    key = jnp.where(valid & jnp.logical_not(forced), imp_t, -1.0)
    sel = jnp.where(valid & forced, 1.0, 0.0)
    for _ in range(n_pick):
        m = jnp.max(key, axis=0, keepdims=True)
        idx = jnp.min(jnp.where(key == m, s_f, 1e6), axis=0, keepdims=True)
        pick = (s_f == idx) & (m >= 0.0)
        sel = jnp.where(pick, 1.0, sel)
        key = jnp.where(pick, -1.0, key)
    return sel


def _unselected(p_sum, agg_t, pos_r, n_pick):
    hi, lo = _split(p_sum)
    imp_t = _dot_nt(agg_t, hi) + _dot_nt(agg_t, lo)
    n_blocks = p_sum.shape[1] * CMP_STRIDE // SLC_LEN
    sel_t = _select_blocks(imp_t, pos_r, n_pick, n_blocks)
    return (1.0 - sel_t.T).astype(BF16)


def _online_update(s, v, m, l, acc, v_transposed=False):
    m_new = jnp.maximum(m, jnp.max(s, axis=-1, keepdims=True))
    alpha = jnp.exp2(m - m_new)
    p = jnp.exp2(s - m_new)
    l_new = alpha * l + jnp.sum(p, axis=-1, keepdims=True)
    pv = _dot_nt(p.astype(BF16), v) if v_transposed else _dot(p.astype(BF16), v)
    return m_new, l_new, alpha * acc + pv


def _softmax_rows(s):
    e = jnp.exp2(s - jnp.max(s, axis=-1, keepdims=True))
    return e * (1.0 / jnp.sum(e, axis=-1, keepdims=True))


def _combine_and_store(o_ref, gsig, o_c, o_s, o_w, nn, qb):
    rows = NSA_HPG * qb
    for g in range(NSA_GROUPS):
        def gcol(j):
            cols = [gsig[:, SM_GATE_OFF + (g * NSA_HPG + h) * 3 + j:SM_GATE_OFF + (g * NSA_HPG + h) * 3 + j + 1]
                    for h in range(NSA_HPG)]
            return jnp.concatenate(cols, axis=0)

        def blk(x):
            return x[g * rows:(g + 1) * rows, g * NSA_HD:(g + 1) * NSA_HD]

        o = gcol(0) * blk(o_c) + gcol(1) * blk(o_s) + gcol(2) * blk(o_w)
        o = o * lax.rsqrt(jnp.mean(o * o, axis=-1, keepdims=True) + RMS_EPS) * nn
        for h in range(NSA_HPG):
            c0 = (g * NSA_HPG + h) * NSA_HD
            o_ref[:, c0:c0 + NSA_HD] = o[h * qb:(h + 1) * qb]


def _rep_lanes(x, n):
    return jnp.concatenate([x] * n, axis=1)


def _softmax_cols(s):
    e = jnp.exp2(s - jnp.max(s, axis=0, keepdims=True))
    return e * (1.0 / jnp.sum(e, axis=0, keepdims=True))


def _nsa_prompt_body(nq_ref, sm_ref, kcv_ref, kvb_ref, ebt_ref, agg_ref, gb_ref, nn_ref, o_ref, *, qb):
    i = pl.program_id(1)
    s0 = i * qb
    pos_r = s0 + lax.broadcasted_iota(jnp.int32, (1, qb), 1)
    cols = NSA_HEADS * qb
    nq_t = (nq_ref[...] * (NSA_HD ** -0.5 * LOG2E)).T
    zero = jnp.zeros((NSA_HD, qb), F32)
    q_rows = []
    for g in range(NSA_GROUPS):
        blks = [nq_t[gh * NSA_HD:(gh + 1) * NSA_HD, :] if gh // NSA_HPG == g else zero for gh in range(NSA_HEADS)]
        q_rows.append(jnp.concatenate(blks, axis=1))
    qbd_t = jnp.concatenate(q_rows, axis=0).astype(BF16)

    nc = kcv_ref.shape[1]
    cmp_end = lax.broadcasted_iota(jnp.int32, (nc, qb), 0) * CMP_STRIDE + (CMP_LEN - 1)
    valid = jnp.where(cmp_end <= pos_r, 1.0, 0.0)
    s_c = _dot(kcv_ref[0, :, 0:LANES], qbd_t) + _rep_lanes(jnp.where(valid > 0.5, 0.0, NEG_BIG), NSA_HEADS)
    e_c = jnp.exp2(s_c - jnp.maximum(jnp.max(s_c, axis=0, keepdims=True), MASKED_MAX))
    p_c = e_c * (1.0 / jnp.maximum(jnp.sum(e_c, axis=0, keepdims=True), 1e-30))
    o_c = _dot_tn(kcv_ref[0, :, LANES:2 * LANES], p_c.astype(BF16))

    unsel = []
    for g in range(NSA_GROUPS):
        p_sum = p_c[:, g * NSA_HPG * qb:(g * NSA_HPG + 1) * qb]
        for h in range(1, NSA_HPG):
            p_sum = p_sum + p_c[:, (g * NSA_HPG + h) * qb:(g * NSA_HPG + h + 1) * qb]
        hi, lo = _split(p_sum)
        imp_t = _dot(agg_ref[...], hi) + _dot(agg_ref[...], lo)
        sel_t = _select_blocks(imp_t, pos_r, SLC_TOP - 3, nc * CMP_STRIDE // SLC_LEN)
        unsel += [(1.0 - sel_t).astype(BF16)] * NSA_HPG
    lhs_t = jnp.concatenate([qbd_t, jnp.concatenate(unsel, axis=1)], axis=0)

    def scores(kt):
        krows = pl.ds(pl.multiple_of(kt * KV_TILE, KV_TILE), KV_TILE)
        rhs = jnp.concatenate([kvb_ref[0, krows, 0:LANES], ebt_ref[krows, :]], axis=1)
        return _dot(rhs, lhs_t)

    def causal_bias(kt):
        key_pos = kt * KV_TILE + lax.broadcasted_iota(jnp.int32, (KV_TILE, qb), 0)
        return _rep_lanes(jnp.where(key_pos <= pos_r, 0.0, NEG_BIG), NSA_HEADS)

    def update(kt, s, carry):
        krows = pl.ds(pl.multiple_of(kt * KV_TILE, KV_TILE), KV_TILE)
        m, l, acc = carry
        m_new = jnp.maximum(m, jnp.max(s, axis=0, keepdims=True))
        alpha = jnp.exp2(m - m_new)
        p = jnp.exp2(s - m_new)
        l_new = alpha * l + jnp.sum(p, axis=0, keepdims=True)
        acc_new = alpha * acc + _dot_tn(kvb_ref[0, krows, LANES:2 * LANES], p.astype(BF16))
        return m_new, l_new, acc_new

    def pair(kp, carry):
        s_a = scores(2 * kp)
        s_b = scores(2 * kp + 1)
        return update(2 * kp + 1, s_b, update(2 * kp, s_a, carry))

    n_tiles = (s0 + qb - 1) // KV_TILE + 1
    n_pairs = (n_tiles - 1) // 2
    last = kvb_ref.shape[1] // KV_TILE - 1
    init = (jnp.full((1, cols), NEG_BIG, F32), jnp.zeros((1, cols), F32), jnp.zeros((LANES, cols), F32))
    carry = lax.fori_loop(0, n_pairs, pair, init)
    kt_a = 2 * n_pairs
    kt_b = jnp.minimum(kt_a + 1, last)
    s_a = scores(kt_a) + causal_bias(kt_a)
    s_b = scores(kt_b) + causal_bias(kt_a + 1)
    _, l, acc = update(kt_b, s_b, update(kt_a, s_a, carry))
    o_s = acc * (1.0 / l)

    w_len = WINDOW + qb
    w_start = pl.multiple_of(jnp.maximum(s0 - WINDOW, 0), qb)
    wrows = pl.ds(w_start, w_len)
    pos_w = w_start + lax.broadcasted_iota(jnp.int32, (w_len, qb), 0)
    ok = (pos_w <= pos_r) & (pos_r - pos_w < WINDOW)
    s_w = _dot(kvb_ref[0, wrows, 2 * LANES:3 * LANES], qbd_t) + _rep_lanes(jnp.where(ok, 0.0, NEG_BIG), NSA_HEADS)
    o_w = _dot_tn(kvb_ref[0, wrows, 3 * LANES:4 * LANES], _softmax_cols(s_w).astype(BF16))

    gsig_t = _sigmoid(sm_ref[...] + gb_ref[...]).T
    for pair in range(NSA_HEADS // 2):
        blocks = []
        for hh in (2 * pair, 2 * pair + 1):
            g = hh // NSA_HPG

            def blk(x):
                return x[g * NSA_HD:(g + 1) * NSA_HD, hh * qb:(hh + 1) * qb]

            def gate(j):
                r = SM_GATE_OFF + hh * 3 + j
                return gsig_t[r:r + 1, :]

            o = gate(0) * blk(o_c) + gate(1) * blk(o_s) + gate(2) * blk(o_w)
            blocks.append(o * lax.rsqrt(jnp.mean(o * o, axis=0, keepdims=True) + RMS_EPS) * nn_ref[...])
        o_ref[:, pair * LANES:(pair + 1) * LANES] = jnp.concatenate(blocks, axis=0).T


def _nsa_prompt(nq, sm, kcv, kvb3, ebt, agg_t, gate_b, nsa_norm, nb, t_len):
    qb = 128
    nqb = t_len // qb
    nc = t_len // CMP_STRIDE
    row = lambda b, i: (b * nqb + i, 0)
    return pl.pallas_call(
        functools.partial(_nsa_prompt_body, qb=qb),
        grid=(nb, nqb),
        in_specs=[pl.BlockSpec((qb, 512), row), pl.BlockSpec((qb, LANES), row),
                  pl.BlockSpec((1, nc, 256), lambda b, i: (b, 0, 0)),
                  pl.BlockSpec((1, t_len, 512), lambda b, i: (b, 0, 0)),
                  pl.BlockSpec((t_len, N_SLC_PAD), lambda b, i: (0, 0)),
                  pl.BlockSpec((N_SLC_PAD, nc), lambda b, i: (0, 0)),
                  pl.BlockSpec((1, LANES), lambda b, i: (0, 0)),
                  pl.BlockSpec((NSA_HD, 1), lambda b, i: (0, 0))],
        out_specs=pl.BlockSpec((qb, 512), row),
        out_shape=jax.ShapeDtypeStruct((nb * t_len, 512), F32),
        compiler_params=_cparams(("parallel", "arbitrary")),
        name="nsa_prompt",
    )(nq, sm, kcv, kvb3, ebt, agg_t, gate_b, nsa_norm)


def _nsa_sample_body(*refs, n_new, past_len, pages_per_tile, w_buf):
    pages = refs[1:1 + pages_per_tile]
    (nq_ref, sm_ref, kcv_ref, new_ref, wnew_ref, win_ref, eb_ref, agg_ref, gb_ref, nn_ref, o_ref,
     lhs_ref, oc_ref, m_ref, l_ref, acc_ref) = refs[1 + pages_per_tile:]
    kt = pl.program_id(1)
    qb = n_new
    rows = NSA_HEADS * qb
    pos_c = past_len + lax.broadcasted_iota(jnp.int32, (qb, 1), 0)

    @pl.when(kt == 0)
    def _():
        qbd = _qbd(nq_ref[...], qb)
        o_c, p_c = _compressed_branch(qbd, kcv_ref[0, :, 0:LANES], kcv_ref[0, :, LANES:2 * LANES], pos_c, qb)
        oc_ref[...] = o_c
        pos_r = past_len + lax.broadcasted_iota(jnp.int32, (1, N_SLC_PAD), 1)
        unsel = []
        for g in range(NSA_GROUPS):
            p_sum = _head_sum(p_c, g, qb)
            p_pad = jnp.concatenate([p_sum, jnp.zeros((N_SLC_PAD - qb, p_sum.shape[1]), F32)], axis=0)
            unsel += [_unselected(p_pad, agg_ref[...], pos_r, SLC_TOP - 3)[0:qb]] * NSA_HPG
        lhs_ref[...] = jnp.concatenate([qbd, jnp.concatenate(unsel, axis=0)], axis=1)
        m_ref[...] = jnp.full((rows, 1), NEG_BIG, F32)
        l_ref[...] = jnp.zeros((rows, 1), F32)
        acc_ref[...] = jnp.zeros((rows, LANES), F32)

    k_t = jnp.concatenate([p[0, 0:LANES, :] for p in pages], axis=1).astype(BF16)
    v_t = jnp.concatenate([p[0, LANES:2 * LANES, :] for p in pages], axis=1).astype(BF16)
    s = _dot(lhs_ref[...], jnp.concatenate([k_t, eb_ref[0]], axis=0))
    m, l, acc = _online_update(s, v_t, m_ref[...], l_ref[...], acc_ref[...], v_transposed=True)
    m_ref[...] = m
    l_ref[...] = l
    acc_ref[...] = acc

    @pl.when(kt == pl.num_programs(1) - 1)
    def _():
        qbd = lhs_ref[:, 0:LANES]
        pad = jnp.zeros((LANES - n_new, LANES), F32)
        q_idx = lax.broadcasted_iota(jnp.int32, (qb, LANES), 0)
        lane = lax.broadcasted_iota(jnp.int32, (qb, LANES), 1)
        new_ok = _rep_rows(jnp.where((lane < n_new) & (lane <= q_idx), 0.0, NEG_BIG), NSA_HEADS)
        new = new_ref[...]
        k_new = jnp.concatenate([new[:, 2 * LANES:3 * LANES], pad], axis=0).astype(BF16)
        v_new = jnp.concatenate([new[:, 3 * LANES:4 * LANES], pad], axis=0).astype(BF16)
        _, l_f, acc_f = _online_update(_dot_nt(qbd, k_new) + new_ok, v_new, m, l, acc)
        o_s = acc_f * (1.0 / l_f)
        pos_w = (past_len - w_buf) + lax.broadcasted_iota(jnp.int32, (1, w_buf), 1)
        ok = (pos_w <= pos_c) & (pos_c - pos_w < WINDOW) & (pos_w >= 0)
        s_w = _dot(qbd, win_ref[0, 0:LANES, :].astype(BF16)) + _rep_rows(jnp.where(ok, 0.0, NEG_BIG), NSA_HEADS)
        wnew = wnew_ref[...]
        kw_new = jnp.concatenate([wnew[:, 0:LANES], pad], axis=0).astype(BF16)
        vw_new = jnp.concatenate([wnew[:, LANES:2 * LANES], pad], axis=0).astype(BF16)
        s_w2 = _dot_nt(qbd, kw_new) + new_ok
        p_w = _softmax_rows(jnp.concatenate([s_w, s_w2], axis=1)).astype(BF16)
        o_w = _dot_nt(p_w[:, 0:w_buf], win_ref[0, LANES:2 * LANES, :].astype(BF16)) + _dot(p_w[:, w_buf:], vw_new)
        gsig = _sigmoid(sm_ref[...] + gb_ref[...])
        _combine_and_store(o_ref, gsig, oc_ref[...], o_s, o_w, nn_ref[...], qb)


def _nsa_sample(nq, sm, kcv, rows_new, win_new, win_t, cache_t, page_table, eb, agg_t, gate_b, nsa_norm,
                nb, n_new, n_pages):
    past_len = n_pages * PAGE_SIZE
    pages_per_tile = min(n_pages, 16)
    n_tiles = n_pages // pages_per_tile
    nc = past_len // CMP_STRIDE
    w_buf = win_t.shape[2]
    rows = NSA_HEADS * n_new

    def page_spec(k):
        return pl.BlockSpec((1, 2 * LANES, PAGE_SIZE), lambda b, t, pt: (pt[b, t * pages_per_tile + k], 1, 0))

    row = lambda b, t, pt: (b, 0)
    const2 = lambda b, t, pt: (0, 0)
    grid_spec = pltpu.PrefetchScalarGridSpec(
        num_scalar_prefetch=1,
        grid=(nb, n_tiles),
        in_specs=[page_spec(k) for k in range(pages_per_tile)] + [
            pl.BlockSpec((n_new, 512), row), pl.BlockSpec((n_new, LANES), row),
            pl.BlockSpec((1, nc, 256), lambda b, t, pt: (b, 0, 0)),
            pl.BlockSpec((n_new, 512), row), pl.BlockSpec((n_new, 256), row),
            pl.BlockSpec((1, 256, w_buf), lambda b, t, pt: (b, 0, 0)),
            pl.BlockSpec((1, N_SLC_PAD, pages_per_tile * PAGE_SIZE), lambda b, t, pt: (t, 0, 0)),
            pl.BlockSpec((N_SLC_PAD, nc), const2),
            pl.BlockSpec((1, LANES), const2), pl.BlockSpec((1, NSA_HD), const2)],
        out_specs=pl.BlockSpec((n_new, 512), row),
        scratch_shapes=[pltpu.VMEM((rows, 2 * LANES), BF16), pltpu.VMEM((rows, LANES), F32),
                        pltpu.VMEM((rows, 1), F32), pltpu.VMEM((rows, 1), F32), pltpu.VMEM((rows, LANES), F32)],
    )
    return pl.pallas_call(
        functools.partial(_nsa_sample_body, n_new=n_new, past_len=past_len, pages_per_tile=pages_per_tile,
                          w_buf=w_buf),
        grid_spec=grid_spec,
        out_shape=jax.ShapeDtypeStruct((nb * n_new, 512), F32),
        compiler_params=_cparams(("parallel", "arbitrary")),
        name="nsa_sample",
    )(page_table, *([cache_t] * pages_per_tile), nq, sm, kcv, rows_new, win_new, win_t, eb, agg_t, gate_b, nsa_norm)


def _route(logits):
    lane = lax.broadcasted_iota(jnp.int32, logits.shape, 1)
    lane_f = lane.astype(F32)
    big = 1e6
    is_g = (lane >= N_EXPERTS) & (lane < N_EXPERTS + N_EXPERT_GROUPS)
    gl = jnp.where(is_g, logits, NEG_BIG)
    gmax = jnp.max(gl, axis=-1, keepdims=True)
    gtop = jnp.min(jnp.where(gl == gmax, lane_f, big), axis=-1, keepdims=True) - N_EXPERTS
    p_group = 1.0 / jnp.sum(jnp.exp(gl - gmax), axis=-1, keepdims=True)
    in_grp = (lane < N_EXPERTS) & (jnp.right_shift(lane, 3).astype(F32) == gtop)
    el = jnp.where(in_grp, logits, NEG_BIG)
    m1 = jnp.max(el, axis=-1, keepdims=True)
    i1 = jnp.min(jnp.where(el == m1, lane_f, big), axis=-1, keepdims=True)
    el2 = jnp.where(lane_f == i1, NEG_BIG, el)
    m2 = jnp.max(el2, axis=-1, keepdims=True)
    i2 = jnp.min(jnp.where(el2 == m2, lane_f, big), axis=-1, keepdims=True)
    r = jnp.exp(m2 - m1)
    t1 = 1.0 / (1.0 + r)
    t2 = r * t1
    return jnp.where(lane_f == i1, p_group * t1, 0.0) + jnp.where(lane_f == i2, p_group * t2, 0.0)


def _mix_body(og_ref, on_ref, x_ref, wo_ref, g_ref, b_ref, rwh_ref, rwl_ref, rb_ref, x1_ref, cmb_ref):
    mix = _dot(og_ref[...].astype(BF16), wo_ref[0:512, :]) + _dot(on_ref[...].astype(BF16), wo_ref[512:1024, :])
    x1 = _layer_norm(DEEPNORM_ALPHA * x_ref[...] + mix, g_ref[...], b_ref[...])
    x1_ref[...] = x1
    logits = _dot3(x1, rwh_ref[...], rwl_ref[...]) + rb_ref[...]
    cmb_ref[...] = _route(logits)


def _mix(o_gla, o_nsa, x2d, wo, g, b, rw_hi, rw_lo, rb):
    n = x2d.shape[0]
    tm = min(n, 512)
    row = lambda i: (i, 0)
    const = lambda i: (0, 0)
    return pl.pallas_call(
        _mix_body,
        grid=(n // tm,),
        in_specs=[pl.BlockSpec((tm, 512), row), pl.BlockSpec((tm, 512), row), pl.BlockSpec((tm, D_MODEL), row),
                  pl.BlockSpec((D_MODEL, D_MODEL), const), pl.BlockSpec((1, D_MODEL), const),
                  pl.BlockSpec((1, D_MODEL), const), pl.BlockSpec((D_MODEL, LANES), const),
                  pl.BlockSpec((D_MODEL, LANES), const), pl.BlockSpec((1, LANES), const)],
        out_specs=[pl.BlockSpec((tm, D_MODEL), row), pl.BlockSpec((tm, LANES), row)],
        out_shape=[jax.ShapeDtypeStruct((n, D_MODEL), F32), jax.ShapeDtypeStruct((n, LANES), F32)],
        compiler_params=_cparams(("parallel",)),
        name="mix",
    )(o_gla, o_nsa, x2d, wo, g, b, rw_hi, rw_lo, rb)


def _moe_body(x1_ref, cmb_ref, tri_ref, wg_ref, wu_ref, wd_ref, g_ref, b_ref, o_ref,
              acc_ref, xb_ref, key_ref, keyt_ref, cmbt_ref, *, chunk):
    j = pl.program_id(1)
    tm = x1_ref.shape[0]

    @pl.when(j == 0)
    def _():
        acc_ref[...] = jnp.zeros_like(acc_ref)
        xb_ref[...] = x1_ref[...].astype(BF16)
        cmb = cmb_ref[...]
        routed = cmb != 0.0
        rank = _dot(tri_ref[...], jnp.where(routed, 1.0, 0.0).astype(BF16))
        key = jnp.where(routed, rank, -1.0)
        key_ref[...] = key
        keyt_ref[...] = key.T
        cmbt_ref[...] = cmb.T

    lane = lax.broadcasted_iota(jnp.int32, (tm, LANES), 1)
    r_io = lax.broadcasted_iota(jnp.int32, (chunk, tm), 0).astype(F32)
    c_io = lax.broadcasted_iota(jnp.int32, (tm, chunk), 1).astype(F32)
    key_rows, w_rows, key_cols, counts = [], [], [], []
    for k in range(MOE_PAIR):
        e = MOE_PAIR * j + k
        key_rows.append(keyt_ref[pl.ds(e, 1), :])
        w_rows.append(cmbt_ref[pl.ds(e, 1), :])
        key_cols.append(jnp.sum(jnp.where(lane == e, key_ref[...], 0.0), axis=-1, keepdims=True))
        counts.append(jnp.sum(jnp.where(key_rows[k] >= 0.0, 1.0, 0.0)).astype(jnp.int32))

    def one_chunk(c, carry):
        base = (c * chunk).astype(F32)
        sels = [jnp.where(key_rows[k] - base == r_io, 1.0, 0.0) for k in range(MOE_PAIR)]
        xe = _dot(jnp.concatenate(sels, axis=0).astype(BF16), xb_ref[...]).astype(BF16)
        ys = []
        for k in range(MOE_PAIR):
            xk = xe[k * chunk:(k + 1) * chunk]
            hdn = _silu(_dot(xk, wg_ref[k])) * _dot(xk, wu_ref[k])
            wc = jnp.sum(sels[k] * w_rows[k], axis=-1, keepdims=True)
            ys.append((wc * _dot(hdn.astype(BF16), wd_ref[k])).astype(BF16))
        sel_t = jnp.concatenate([jnp.where(key_cols[k] - base == c_io, 1.0, 0.0) for k in range(MOE_PAIR)],
                                axis=1).astype(BF16)
        acc_ref[...] += _dot(sel_t, jnp.concatenate(ys, axis=0))
        return carry

    n_chunks = (functools.reduce(jnp.maximum, counts) + chunk - 1) // chunk
    lax.fori_loop(0, n_chunks, one_chunk, 0)

    @pl.when(j == pl.num_programs(1) - 1)
    def _():
        o_ref[...] = _layer_norm(DEEPNORM_ALPHA * x1_ref[...] + acc_ref[...], g_ref[...], b_ref[...])


def _moe(x1, cmb, wg, wu, wd, g, b):
    n = x1.shape[0]
    tm = min(n, MOE_TILE)
    chunk = min(tm, MOE_CHUNK)
    tri = jnp.where(jnp.arange(tm)[:, None] > jnp.arange(tm)[None, :], 1.0, 0.0).astype(BF16)
    row = lambda i, j: (i, 0)
    const = lambda i, j: (0, 0)
    return pl.pallas_call(
        functools.partial(_moe_body, chunk=chunk),
        grid=(n // tm, N_EXPERTS // MOE_PAIR),
        in_specs=[pl.BlockSpec((tm, D_MODEL), row), pl.BlockSpec((tm, LANES), row),
                  pl.BlockSpec((tm, tm), const),
                  pl.BlockSpec((MOE_PAIR, D_MODEL, D_EXPERT), lambda i, j: (j, 0, 0)),
                  pl.BlockSpec((MOE_PAIR, D_MODEL, D_EXPERT), lambda i, j: (j, 0, 0)),
                  pl.BlockSpec((MOE_PAIR, D_EXPERT, D_MODEL), lambda i, j: (j, 0, 0)),
                  pl.BlockSpec((1, D_MODEL), const), pl.BlockSpec((1, D_MODEL), const)],
        out_specs=pl.BlockSpec((tm, D_MODEL), row),
        out_shape=jax.ShapeDtypeStruct((n, D_MODEL), F32),
        scratch_shapes=[pltpu.VMEM((tm, D_MODEL), F32), pltpu.VMEM((tm, D_MODEL), BF16),
                        pltpu.VMEM((tm, LANES), F32), pltpu.VMEM((LANES, tm), F32), pltpu.VMEM((LANES, tm), F32)],
        compiler_params=_cparams(("parallel", "arbitrary")),
        name="moe",
    )(x1, cmb, tri, wg, wu, wd, g, b)


def _prep_weights(w_in, gla_w_a, gla_b_a, nsa_gate_b, cmp_pe, cmp_w1, cmp_w2, gla_norm, nsa_norm, w_o,
                  ln1_g, ln1_b, router_g_w, router_g_b, router_e_w, router_e_b, w_gate, w_up, w_down,
                  ln2_g, ln2_b, n_cmp):
    o_glr, o_gog, o_nq, o_nkv, o_ngt = 1024, 1040, 1552, 2064, 2832
    pad = jnp.zeros((D_MODEL, LANES - GLA_RANK - 3 * NSA_HEADS), F32)
    w_proj = jnp.concatenate([w_in[:, 0:512], w_in[:, 512:1024], w_in[:, o_gog:o_gog + 512],
                              w_in[:, o_nq:o_nq + 512], w_in[:, o_nkv:o_nkv + 768],
                              w_in[:, o_glr:o_glr + GLA_RANK], w_in[:, o_ngt:o_ngt + 24], pad], axis=1).astype(BF16)
    wa = jnp.zeros((LANES, GLA_HEADS * GLA_DK), F32).at[0:GLA_RANK].set(gla_w_a)
    wa_hi, wa_lo = _split(wa)
    ba = gla_b_a.reshape(1, -1)
    gate_b = jnp.zeros((1, LANES), F32).at[0, SM_GATE_OFF:SM_GATE_OFF + 24].set(nsa_gate_b)
    eye = jnp.eye(NSA_GROUPS, dtype=F32)
    w1r = cmp_w1.reshape(2, CMP_LEN, NSA_HD, CMP_HIDDEN)

    def split_w1(w):
        wa_ = jnp.einsum('ldj,gh->lgdhj', w[:CMP_STRIDE], eye).reshape(CMP_STRIDE * LANES, 2 * CMP_HIDDEN)
        wb_ = jnp.einsum('ldj,gh->lgdhj', w[CMP_STRIDE:], eye).reshape(CMP_STRIDE * LANES, 2 * CMP_HIDDEN)
        return jnp.concatenate([wa_, wb_], axis=1).astype(BF16)

    wk, wv = split_w1(w1r[0]), split_w1(w1r[1])
    pe = cmp_pe.reshape(2, CMP_LEN * NSA_HD)
    w1_hi, w1_lo = _split(cmp_w1)
    w2bd = jnp.einsum('sjd,gh->sgjhd', cmp_w2, eye).reshape(2, 2 * CMP_HIDDEN, LANES).astype(BF16)
    cw = (wk, wv, pe, w1_hi, w1_lo, w2bd)
    starts = np.arange(n_cmp) * CMP_STRIDE
    sel_starts = np.arange(N_SLC_PAD) * SLC_LEN
    agg_t = ((starts[None, :] < sel_starts[:, None] + SLC_LEN)
             & (starts[None, :] + CMP_LEN > sel_starts[:, None])
             & (np.arange(n_cmp)[None, :] < n_cmp - 1))
    agg_t = jnp.asarray(agg_t, BF16)
    rw = jnp.zeros((D_MODEL, LANES), F32)
    rw = rw.at[:, 0:N_EXPERTS].set(router_e_w.transpose(1, 0, 2).reshape(D_MODEL, N_EXPERTS))
    rw = rw.at[:, N_EXPERTS:N_EXPERTS + N_EXPERT_GROUPS].set(router_g_w)
    rw_hi, rw_lo = _split(rw)
    rb = jnp.zeros((1, LANES), F32).at[0, 0:N_EXPERTS].set(router_e_b.reshape(-1))
    rb = rb.at[0, N_EXPERTS:N_EXPERTS + N_EXPERT_GROUPS].set(router_g_b)
    return dict(w_proj=w_proj, wa_hi=wa_hi, wa_lo=wa_lo, ba=ba, gate_b=gate_b, cw=cw, agg_t=agg_t,
                gn=gla_norm.reshape(1, -1), nn=nsa_norm.reshape(1, -1), wo=w_o.astype(BF16),
                ln1_g=ln1_g.reshape(1, -1), ln1_b=ln1_b.reshape(1, -1), rw_hi=rw_hi, rw_lo=rw_lo, rb=rb,
                wg=w_gate.astype(BF16), wu=w_up.astype(BF16), wd=w_down.astype(BF16),
                ln2_g=ln2_g.reshape(1, -1), ln2_b=ln2_b.reshape(1, -1))


def _tail(w, o_gla, o_nsa, x2d):
    x1, cmb = _mix(o_gla, o_nsa, x2d, w['wo'], w['ln1_g'], w['ln1_b'], w['rw_hi'], w['rw_lo'], w['rb'])
    return _moe(x1, cmb, w['wg'], w['wu'], w['wd'], w['ln2_g'], w['ln2_b'])


def kernel(x_prompt, x_sample, cache_nsa, state_win, state_gla, page_table, w_in, gla_w_a, gla_b_a, nsa_gate_b,
           cmp_pe, cmp_w1, cmp_w2, gla_norm, nsa_norm, w_o, ln1_g, ln1_b, router_g_w, router_g_b, router_e_w,
           router_e_b, w_gate, w_up, w_down, ln2_g, ln2_b):
    assert w_in.shape[0] == 1, "single layer"
    bp, tp, _ = x_prompt.shape
    bs, ts, _ = x_sample.shape
    n_pages = page_table.shape[1]
    past_len = n_pages * PAGE_SIZE
    assert tp == past_len, "prompt and past share the compressed-block count"
    w = _prep_weights(w_in[0], gla_w_a[0], gla_b_a[0], nsa_gate_b[0], cmp_pe[0], cmp_w1[0], cmp_w2[0],
                      gla_norm[0], nsa_norm[0], w_o[0], ln1_g[0], ln1_b[0], router_g_w[0], router_g_b[0],
                      router_e_w[0], router_e_b[0], w_gate[0], w_up[0], w_down[0], ln2_g[0], ln2_b[0],
                      n_cmp=tp // CMP_STRIDE)

    xp2 = x_prompt.reshape(bp * tp, D_MODEL)
    qk, v, og, nq, rows4, win, kvb, sm, rows_t = _proj(xp2, w['w_proj'], seq_len=tp)
    o_gla, gla_p = _gla(qk, v, og, sm, w['wa_hi'], w['wa_lo'], w['ba'], w['gn'], None, bp, tp)
    kcv = _compress_prompt(rows4, w['cw'], bp, tp)
    blk_of_key = jnp.arange(tp, dtype=jnp.int32) // SLC_LEN
    ebt = jnp.where(blk_of_key[:, None] == jnp.arange(N_SLC_PAD, dtype=jnp.int32)[None, :], NEG_BIG, 0.0).astype(BF16)
    o_nsa = _nsa_prompt(nq, sm, kcv, kvb.reshape(bp, tp, 512), ebt, w['agg_t'], w['gate_b'], w['nn'].reshape(-1, 1),
                        bp, tp)
    y_p = _tail(w, o_gla, o_nsa, xp2).reshape(bp, tp, D_MODEL)
    w_rows = min(WINDOW, tp)
    rows_p = rows_t.reshape(bp, 4, NSA_GROUPS, NSA_HD, tp).transpose(0, 4, 1, 2, 3)[None]
    win_p = win.reshape(bp, tp, 2, NSA_GROUPS, NSA_HD)[None, :, tp - w_rows:]

    xs2 = x_sample.reshape(bs * ts, D_MODEL)
    qk, v, og, nq, rows4s, win_new, _, sm = _proj(xs2, w['w_proj'])
    o_gla, gla_s = _gla(qk, v, og, sm, w['wa_hi'], w['wa_lo'], w['ba'], w['gn'], state_gla[0], bs, ts)
    cache_t = cache_nsa[0].transpose(0, 2, 3, 4, 1).reshape(-1, 4 * NSA_GROUPS * NSA_HD, PAGE_SIZE)
    w_buf = state_win.shape[2]
    win_t = state_win[0].transpose(0, 2, 3, 4, 1).reshape(bs, 2 * NSA_GROUPS * NSA_HD, w_buf)
    kcv = _compress_sample(cache_t, page_table, w['cw'], bs, n_pages)
    tile_keys = min(n_pages, 16) * PAGE_SIZE
    eb = jnp.where(blk_of_key.reshape(-1, 1, tile_keys) == jnp.arange(N_SLC_PAD, dtype=jnp.int32)[None, :, None],
                   NEG_BIG, 0.0).astype(BF16)
    o_nsa = _nsa_sample(nq, sm, kcv, rows4s, win_new, win_t, cache_t, page_table, eb, w['agg_t'], w['gate_b'],
                        w['nn'], bs, ts, n_pages)
    y_s = _tail(w, o_gla, o_nsa, xs2).reshape(bs, ts, D_MODEL)
    rows_s = rows4s.reshape(1, bs, ts, 4, NSA_GROUPS, NSA_HD)
    win_s = jnp.concatenate([state_win[0][:, ts:], win_new.reshape(bs, ts, 2, NSA_GROUPS, NSA_HD)], axis=1)[None]

    return (y_p, y_s, rows_p, rows_s, win_p, win_s, gla_p[None], gla_s[None])
```
